```python
import jax, jax.numpy as jnp
from jax import lax
import numpy as np

D_MODEL = 1024
BATCH = 8
SEQ = 8192
DEPTH = 4
DEC_BATCH = 32
DEC_SEQ = 16
PAST_LEN = 2048

CHUNK = 64
Q_BLOCK = 128
CONV_W = 3
FOX_HEADS = 8
FOX_HEAD_DIM = 64
FOX_WIDTH = FOX_HEADS * FOX_HEAD_DIM
FOX_FORGET_BIAS = 2.0
SC_WIDTH = 512
GLA_HEADS = 4
GLA_DK = 128
GLA_DV = 128
GLA_KW = GLA_HEADS * GLA_DK
GLA_VW = GLA_HEADS * GLA_DV
GLA_RANK = 16
GLA_TAU = 16.0
N_BRANCH = 3
BRANCH_WIDTH = 512
D_FF = 2816
DEEPNORM_ALPHA = (2.0 * DEPTH) ** 0.25
DEEPNORM_BETA = (8.0 * DEPTH) ** -0.25
ADA_SCALE = 0.5
LN_EPS = 1e-5
F32 = jnp.float32

IN_SPLITS = [FOX_WIDTH, FOX_WIDTH, FOX_WIDTH, FOX_HEADS,
             SC_WIDTH, SC_WIDTH, SC_WIDTH,
             GLA_KW, GLA_KW, GLA_VW, GLA_VW, GLA_RANK,
             D_MODEL, D_MODEL, D_MODEL]
IN_COLS = sum(IN_SPLITS)
IN_OFFSETS = tuple(int(o) for o in np.cumsum(IN_SPLITS)[:-1])

kernel_name = 'hybrid_fox_shortconv_gla_convffn_step'


def layer_norm(x, g=None, b=None):
    xf = x.astype(F32)
    mu = jnp.mean(xf, axis=-1, keepdims=True)
    var = jnp.mean(jnp.square(xf - mu), axis=-1, keepdims=True)
    y = (xf - mu) * lax.rsqrt(var + LN_EPS)
    if g is not None:
        y = y * g.astype(F32) + b.astype(F32)
    return y.astype(x.dtype)


def causal_dwconv(u, prev, w, b):
    L = u.shape[1]
    ext = jnp.concatenate([prev.astype(u.dtype), u], axis=1)
    y = b + ext[:, 0:L] * w[0]
    for i in range(1, CONV_W):
        y = y + ext[:, i:i + L] * w[i]
    return y, ext[:, L:]


def fox_attend(q, k, v, cq, ckT, qpos, kpos):
    s = jnp.einsum('bqhd,bkhd->bhqk', q, k) * (FOX_HEAD_DIM ** -0.5)
    s = s + cq.transpose(0, 2, 1)[..., None] - ckT[:, :, None, :]
    s = jnp.where(kpos[None, None, None, :] <= qpos[None, None, :, None], s, -jnp.inf)
    p = jax.nn.softmax(s, axis=-1)
    return jnp.einsum('bhqk,bkhd->bqhd', p, v)


def fox_prompt(q, k, v, logf):
    B, S, H, Dh = q.shape
    nb = S // Q_BLOCK
    c = jnp.cumsum(logf, axis=1)
    ckT = c.transpose(0, 2, 1)
    kpos = jnp.arange(S)
    qb = q.reshape(B, nb, Q_BLOCK, H, Dh).transpose(1, 0, 2, 3, 4)
    cb = c.reshape(B, nb, Q_BLOCK, H).transpose(1, 0, 2, 3)
    pb = kpos.reshape(nb, Q_BLOCK)

    def one_block(blk):
        qi, ci, pi = blk
        return fox_attend(qi, k, v, ci, ckT, pi, kpos)

    o = lax.map(one_block, (qb, cb, pb))
    return o.transpose(1, 0, 2, 3, 4).reshape(B, S, H * Dh)


def gla_chunk(S, q, k, v, la):
    L = q.shape[2]
    b = jnp.cumsum(la, axis=2)
    o_inter = jnp.einsum('bhtk,bhkv->bhtv', q * jnp.exp(b), S)
    causal = jnp.tril(jnp.ones((L, L), dtype=bool))
    diff = b[:, :, :, None, :] - b[:, :, None, :, :]
    decay = jnp.exp(jnp.where(causal[None, None, :, :, None], diff, -jnp.inf))
    scores = jnp.einsum('bhtk,bhsk,bhtsk->bhts', q, k, decay)
    o = o_inter + jnp.einsum('bhts,bhsv->bhtv', scores, v)
    b_last = b[:, :, -1:, :]
    S_new = jnp.exp(b_last[:, :, 0, :, None]) * S + jnp.einsum('bhsk,bhsv->bhkv', k * jnp.exp(b_last - b), v)
    return o, S_new


def gla_sequence(S0, q, k, v, la):
    B, H, L, _ = q.shape
    if L <= CHUNK:
        return gla_chunk(S0, q, k, v, la)
    nc = L // CHUNK

    def to_chunks(t):
        return t.reshape(B, H, nc, CHUNK, t.shape[-1]).transpose(2, 0, 1, 3, 4)

    def step(S, xs):
        o, S = gla_chunk(S, *xs)
        return S, o

    S_fin, o = lax.scan(step, S0, (to_chunks(q), to_chunks(k), to_chunks(v), to_chunks(la)))
    return o.transpose(1, 2, 0, 3, 4).reshape(B, H, L, v.shape[-1]), S_fin


def token_mixers(h, p, fox_cache, sc_prev, gla_S0):
    B, L, _ = h.shape
    (fq, fk, fv, ff, sb, scg, sx, gq, gk, gv, gg, glr, m_a, m_b, m_c) = jnp.split(h @ p['w_in'], IN_OFFSETS, axis=-1)
    q = fq.reshape(B, L, FOX_HEADS, FOX_HEAD_DIM)
    k = fk.reshape(B, L, FOX_HEADS, FOX_HEAD_DIM)
    v = fv.reshape(B, L, FOX_HEADS, FOX_HEAD_DIM)
    logf = jax.nn.log_sigmoid((ff + p['b_fox_f']).astype(F32))
    if fox_cache is None:
        y_fox = fox_prompt(q.astype(F32), k.astype(F32), v.astype(F32), logf)
    else:
        ck, cv, cl = fox_cache
        P = ck.shape[1]
        k_all = jnp.concatenate([ck.astype(F32), k.astype(F32)], axis=1)
        v_all = jnp.concatenate([cv.astype(F32), v.astype(F32)], axis=1)
        c_all = jnp.cumsum(jnp.concatenate([cl.astype(F32), logf], axis=1), axis=1)
        kpos = jnp.arange(P + L)
        y_fox = fox_attend(q.astype(F32), k_all, v_all, c_all[:, P:], c_all.transpose(0, 2, 1),
                           kpos[P:], kpos).reshape(B, L, FOX_WIDTH)
    y_fox = y_fox.astype(h.dtype)
    y_sc, sc_new = causal_dwconv(scg * sx, sc_prev, p['sc_conv_w'], p['sc_conv_b'])
    y_sc = sb * y_sc
    def heads(t, d):
        return t.reshape(B, L, GLA_HEADS, d).transpose(0, 2, 1, 3).astype(F32)
    la = jax.nn.log_sigmoid((glr @ p['w_gla_up'] + p['b_gla_a']).astype(F32)) / GLA_TAU
    o, gla_new = gla_sequence(gla_S0.astype(F32), heads(gq, GLA_DK) * (GLA_DK ** -0.5),
                              heads(gk, GLA_DK), heads(gv, GLA_DV), heads(la, GLA_DK))
    o = o * lax.rsqrt(jnp.mean(jnp.square(o), axis=-1, keepdims=True) + LN_EPS) * p['gla_norm_g'].astype(F32)
    y_gla = o.transpose(0, 2, 1, 3).reshape(B, L, GLA_VW).astype(h.dtype) * jax.nn.silu(gg)
    wb = p['w_branch']
    merged = (jax.nn.sigmoid(m_a) * (y_fox @ wb[0])
              + jax.nn.sigmoid(m_b) * (y_sc @ wb[1])
              + jax.nn.sigmoid(m_c) * (y_gla @ wb[2]))
    return merged @ p['w_out'], (k, v, logf, sc_new, gla_new)


def conv_ffn(h, p, prev):
    up = h @ p['w_up']
    ug, uv = jnp.split(up, [D_FF], axis=-1)
    ugc, new_prev = causal_dwconv(ug, prev, p['ffn_conv_w'], p['ffn_conv_b'])
    return (jax.nn.gelu(ugc) * uv) @ p['w_down'], new_prev


def trunk_layer(x, cond, p, fox_cache, sc_prev, gla_S0, ffn_prev):
    mod = jax.nn.silu(cond) @ p['w_ada'] + p['b_ada']
    sh1, sc1, g1, sh2, sc2, g2 = jnp.split(mod[:, None, :], 6, axis=-1)
    h = layer_norm(x) * (1.0 + sc1) + sh1
    mix, (k, v, logf, sc_new, gla_new) = token_mixers(h, p, fox_cache, sc_prev, gla_S0)
    x = layer_norm(DEEPNORM_ALPHA * x + g1 * mix, p['ln1_g'], p['ln1_b'])
    h = layer_norm(x) * (1.0 + sc2) + sh2
    f, ffn_new = conv_ffn(h, p, ffn_prev)
    x = layer_norm(DEEPNORM_ALPHA * x + g2 * f, p['ln2_g'], p['ln2_b'])
    return x, (k, v, logf, sc_new, gla_new, ffn_new)


def setup_inputs(seed: int = 0) -> dict:
    key = jax.random.key(seed)
    ks = iter(jax.random.split(key, 32))

    def nrm(shape, scale=1.0):
        return jax.random.normal(next(ks), shape, F32) * scale

    d = D_MODEL
    return {
        'x_prompt': nrm((BATCH, SEQ, d)),
        'x_sample': nrm((DEC_BATCH, DEC_SEQ, d)),
        'c_prompt': nrm((BATCH, d)),
        'c_sample': nrm((DEC_BATCH, d)),
        'cache_fox_k': nrm((DEPTH, DEC_BATCH, PAST_LEN, FOX_HEADS, FOX_HEAD_DIM)),
        'cache_fox_v': nrm((DEPTH, DEC_BATCH, PAST_LEN, FOX_HEADS, FOX_HEAD_DIM)),
        'cache_fox_logf': jax.nn.log_sigmoid(FOX_FORGET_BIAS + nrm((DEPTH, DEC_BATCH, PAST_LEN, FOX_HEADS))),
        'state_shortconv': nrm((DEPTH, DEC_BATCH, CONV_W - 1, SC_WIDTH)),
        'state_gla': nrm((DEPTH, DEC_BATCH, GLA_HEADS, GLA_DK, GLA_DV), 0.5),
        'state_ffn_conv': nrm((DEPTH, DEC_BATCH, CONV_W - 1, D_FF)),
        'w_ada': nrm((DEPTH, d, 6 * d), ADA_SCALE * d ** -0.5),
        'b_ada': nrm((DEPTH, 6 * d), 0.01),
        'w_in': nrm((DEPTH, d, IN_COLS), d ** -0.5),
        'b_fox_f': FOX_FORGET_BIAS + nrm((DEPTH, FOX_HEADS), 0.1),
        'w_gla_up': nrm((DEPTH, GLA_RANK, GLA_KW), GLA_RANK ** -0.5),
        'b_gla_a': nrm((DEPTH, GLA_KW), 0.1),
        'gla_norm_g': 1.0 + nrm((DEPTH, GLA_DV), 0.01),
        'sc_conv_w': nrm((DEPTH, CONV_W, SC_WIDTH), CONV_W ** -0.5),
        'sc_conv_b': nrm((DEPTH, SC_WIDTH), 0.01),
        'w_branch': nrm((DEPTH, N_BRANCH, BRANCH_WIDTH, d), DEEPNORM_BETA * BRANCH_WIDTH ** -0.5),
        'w_out': nrm((DEPTH, d, d), DEEPNORM_BETA * d ** -0.5),
        'ln1_g': 1.0 + nrm((DEPTH, d), 0.01),
        'ln1_b': nrm((DEPTH, d), 0.01),
        'w_up': nrm((DEPTH, d, 2 * D_FF), d ** -0.5),
        'ffn_conv_w': nrm((DEPTH, CONV_W, D_FF), CONV_W ** -0.5),
        'ffn_conv_b': nrm((DEPTH, D_FF), 0.01),
        'w_down': nrm((DEPTH, D_FF, d), DEEPNORM_BETA * D_FF ** -0.5),
        'ln2_g': 1.0 + nrm((DEPTH, d), 0.01),
        'ln2_b': nrm((DEPTH, d), 0.01),
    }


def reference(x_prompt, x_sample, c_prompt, c_sample, cache_fox_k, cache_fox_v, cache_fox_logf,
              state_shortconv, state_gla, state_ffn_conv, w_ada, b_ada, w_in, b_fox_f, w_gla_up,
              b_gla_a, gla_norm_g, sc_conv_w, sc_conv_b, w_branch, w_out, ln1_g, ln1_b, w_up,
              ffn_conv_w, ffn_conv_b, w_down, ln2_g, ln2_b):
    xp, xs = x_prompt, x_sample
    bp = x_prompt.shape[0]
    outs_p = [[] for _ in range(6)]
    outs_s = [[] for _ in range(6)]
    for l in range(DEPTH):
        p = {'w_ada': w_ada[l], 'b_ada': b_ada[l], 'w_in': w_in[l], 'b_fox_f': b_fox_f[l],
             'w_gla_up': w_gla_up[l], 'b_gla_a': b_gla_a[l], 'gla_norm_g': gla_norm_g[l],
             'sc_conv_w': sc_conv_w[l], 'sc_conv_b': sc_conv_b[l], 'w_branch': w_branch[l],
             'w_out': w_out[l], 'ln1_g': ln1_g[l], 'ln1_b': ln1_b[l], 'w_up': w_up[l],
             'ffn_conv_w': ffn_conv_w[l], 'ffn_conv_b': ffn_conv_b[l], 'w_down': w_down[l],
             'ln2_g': ln2_g[l], 'ln2_b': ln2_b[l]}
        xp, st_p = trunk_layer(xp, c_prompt, p, None,
                               jnp.zeros((bp, CONV_W - 1, SC_WIDTH), xp.dtype),
                               jnp.zeros((bp, GLA_HEADS, GLA_DK, GLA_DV), F32),
                               jnp.zeros((bp, CONV_W - 1, D_FF), xp.dtype))
        xs, st_s = trunk_layer(xs, c_sample, p, (cache_fox_k[l], cache_fox_v[l], cache_fox_logf[l]),
                               state_shortconv[l], state_gla[l], state_ffn_conv[l])
        for i in range(6):
            outs_p[i].append(st_p[i])
            outs_s[i].append(st_s[i])
    fox_k_prompt, fox_v_prompt, fox_logf_prompt, shortconv_prompt, gla_prompt, ffn_conv_prompt = (jnp.stack(o) for o in outs_p)
    fox_k_sample, fox_v_sample, fox_logf_sample, shortconv_sample, gla_sample, ffn_conv_sample = (jnp.stack(o) for o in outs_s)
    return (xp, xs, fox_k_prompt, fox_v_prompt, fox_logf_prompt, shortconv_prompt, gla_prompt, ffn_conv_prompt,
            fox_k_sample, fox_v_sample, fox_logf_sample, shortconv_sample, gla_sample, ffn_conv_sample)
```

```python
import functools

import jax
import jax.numpy as jnp
from jax import lax
from jax.experimental import pallas as pl
from jax.experimental.pallas import tpu as pltpu

F32 = jnp.float32
BF16 = jnp.bfloat16
LN_EPS = 1e-5
GLA_TAU = 16.0
LANES = 128
SUBLANES = 8
VMEM_LIMIT = 56 * 1024 * 1024
HIGHEST = lax.Precision.HIGHEST

TM_PROJ = 512
TM_MERGE = 512
TM_FFN = 512
TQ_FOX = 512
T_GLA = 512
GLA_CHUNK = 128
FFN_CHUNK = 256
ADA_TN = 1024

_NT = (((1,), (1,)), ((), ()))
_TN = (((0,), (0,)), ((), ()))


def _tile(n, pref):
    if n <= pref:
        return n
    t = pref
    while n % t:
        t //= 2
    return t


def _params(*sem):
    return pltpu.CompilerParams(dimension_semantics=sem, vmem_limit_bytes=VMEM_LIMIT)


def _resident(block, index_map):
    return pl.BlockSpec(block, index_map, pipeline_mode=pl.Buffered(1))


def _log_sigmoid(x):
    return jnp.minimum(x, 0.0) - jnp.log1p(jnp.exp(-jnp.abs(x)))


def _ln(x):
    mu = jnp.mean(x, axis=-1, keepdims=True)
    xc = x - mu
    var = jnp.mean(xc * xc, axis=-1, keepdims=True)
    return xc * lax.rsqrt(var + LN_EPS)


def _causal_conv3(u, p0, p1, w_ref, b_ref, cols):
    tm = u.shape[0]
    row = lax.broadcasted_iota(jnp.int32, (tm, 1), 0)
    u1 = jnp.where(row == 0, p1, pltpu.roll(u, 1, 0))
    u2 = jnp.where(row == 0, p0, jnp.where(row == 1, p1, pltpu.roll(u, 2, 0)))
    y = b_ref[:, cols] + u2 * w_ref[0:1, cols]
    y = y + u1 * w_ref[1:2, cols]
    return y + u * w_ref[2:3, cols]


def _ada_kernel(c_ref, w_ref, b_ref, o_ref):
    c = c_ref[...]
    s = (c * jax.nn.sigmoid(c)).astype(BF16)
    o_ref[...] = jnp.dot(s, w_ref[...].astype(BF16), preferred_element_type=F32) + b_ref[...]


def _ada(c_all, w_ada, b_ada):
    depth, d, n6 = w_ada.shape
    r = c_all.shape[0]
    tn = _tile(n6, ADA_TN)
    return pl.pallas_call(
        _ada_kernel,
        grid=(depth, n6 // tn),
        in_specs=[pl.BlockSpec((r, d), lambda l, n: (0, 0)),
                  pl.BlockSpec((None, d, tn), lambda l, n: (l, 0, n)),
                  pl.BlockSpec((None, 1, tn), lambda l, n: (l, 0, n))],
        out_specs=pl.BlockSpec((None, r, tn), lambda l, n: (l, 0, n)),
        out_shape=jax.ShapeDtypeStruct((depth, r, n6), F32),
        compiler_params=_params("arbitrary", "arbitrary"),
        name="ada_mod",
    )(c_all, w_ada, b_ada.reshape(depth, 1, n6))


def _inproj_kernel(x_ref, mod_ref, w_ref, wff_ref, wglr_ref, bff_ref, wup_ref, ba_ref, scw_ref, scb_ref,
                   prev_ref,
                   q_ref, kf_ref, vf_ref, kb_ref, vb_ref, lf_ref, ysc_ref, scst_ref,
                   gq_ref, gk_ref, gv_ref, gg_ref, la_ref,
                   carry_ref, *, d, fw, q_scale, gq_scale):
    j = pl.program_id(1)
    tm = x_ref.shape[0]
    h = _ln(x_ref[...]) * (1.0 + mod_ref[:, d:2 * d]) + mod_ref[:, 0:d]
    hb = h.astype(BF16)

    def proj(g):
        return jnp.dot(hb, w_ref[:, g * fw:(g + 1) * fw], preferred_element_type=F32)

    q_ref[...] = (proj(0) * q_scale).astype(BF16)
    k = proj(1)
    kf_ref[...] = k
    kb_ref[...] = k.astype(BF16)
    v = proj(2)
    vf_ref[...] = v
    vb_ref[...] = v.astype(BF16)
    fft = lax.dot_general(wff_ref[...], hb, _NT, preferred_element_type=F32)
    lf_ref[...] = _log_sigmoid(fft + bff_ref[...])

    first = j == 0
    p0 = jnp.where(first, prev_ref[0:1, :], carry_ref[SUBLANES - 2:SUBLANES - 1, :])
    p1 = jnp.where(first, prev_ref[1:2, :], carry_ref[SUBLANES - 1:SUBLANES, :])
    sb = proj(3)
    u = proj(4) * proj(5)
    y = _causal_conv3(u, p0, p1, scw_ref, scb_ref, slice(None))
    ysc_ref[...] = (sb * y).astype(BF16)
    carry_ref[...] = u[tm - SUBLANES:tm, :]
    scst_ref[...] = carry_ref[SUBLANES - 2:SUBLANES, :]

    gq_ref[...] = (proj(6) * gq_scale).astype(BF16)
    gk_ref[...] = proj(7).astype(BF16)
    gv_ref[...] = proj(8).astype(BF16)
    gg = proj(9)
    gg_ref[...] = (gg * jax.nn.sigmoid(gg)).astype(BF16)
    glr = jnp.dot(hb, wglr_ref[...], preferred_element_type=F32)
    lap = jnp.dot(glr.astype(BF16), wup_ref[...], preferred_element_type=F32) + ba_ref[...]
    la_ref[...] = _log_sigmoid(lap) / GLA_TAU


def _inproj(l, x2d, mod, wp, sc_prev, nb, seq, hd, dk):
    n, d = x2d.shape
    fw = wp["fw"]
    heads = fw // hd
    tm = _tile(seq, TM_PROJ)
    nt = seq // tm
    rows = lambda b, j: (b * nt + j, 0)
    wl = lambda b, j: (l, 0, 0)
    act = lambda dt: jax.ShapeDtypeStruct((n, fw), dt)
    act_spec = pl.BlockSpec((tm, fw), rows)
    kern = functools.partial(_inproj_kernel, d=d, fw=fw, q_scale=hd ** -0.5, gq_scale=dk ** -0.5)
    return pl.pallas_call(
        kern,
        grid=(nb, nt),
        in_specs=[pl.BlockSpec((tm, d), rows),
                  pl.BlockSpec((None, 1, 6 * d), lambda b, j: (b, 0, 0)),
                  _resident((None, d, 10 * fw), wl),
                  _resident((None, heads, d), wl),
                  _resident((None, d, LANES), wl),
                  _resident((None, heads, 1), wl),
                  _resident((None, LANES, fw), wl),
                  _resident((None, 1, fw), wl),
                  _resident((None, 3, fw), wl),
                  _resident((None, 1, fw), wl),
                  pl.BlockSpec((None, 2, fw), lambda b, j: (b, 0, 0))],
        out_specs=[act_spec, act_spec, act_spec, act_spec, act_spec,
                   pl.BlockSpec((None, heads, tm), lambda b, j: (b * nt + j, 0, 0)),
                   act_spec,
                   pl.BlockSpec((None, 2, fw), lambda b, j: (b, 0, 0)),
                   act_spec, act_spec, act_spec, act_spec, act_spec],
        out_shape=[act(BF16), act(F32), act(F32), act(BF16), act(BF16),
                   jax.ShapeDtypeStruct((nb * nt, heads, tm), F32),
                   act(BF16),
                   jax.ShapeDtypeStruct((nb, 2, fw), F32),
                   act(BF16), act(BF16), act(BF16), act(BF16), act(F32)],
        scratch_shapes=[pltpu.VMEM((SUBLANES, fw), F32)],
        compiler_params=_params("arbitrary", "arbitrary"),
        name="in_proj",
    )(x2d, mod, wp["w_main"], wp["w_fft"], wp["w_glr"], wp["b_fox"], wp["w_gla_up"], wp["b_gla"],
      wp["sc_w"], wp["sc_b"], sc_prev)


def _lane_cumsum(x, lane):
    for k in range(7):
        s = 1 << k
        x = x + jnp.where(lane >= s, pltpu.roll(x, s, 1), 0.0)
    return x


def _cumsum_kernel(x_ref, o_ref):
    r, length = x_ref.shape
    lane = lax.broadcasted_iota(jnp.int32, (r, LANES), 1)

    def body(i, carry):
        off = pl.multiple_of(i * LANES, LANES)
        x = _lane_cumsum(x_ref[:, pl.ds(off, LANES)], lane) + carry
        o_ref[:, pl.ds(off, LANES)] = x
        return x[:, LANES - 1:LANES]

    lax.fori_loop(0, length // LANES, body, jnp.zeros((r, 1), F32))


def _cumsum_rows(x):
    r, length = x.shape
    return pl.pallas_call(
        _cumsum_kernel,
        grid=(1,),
        in_specs=[pl.BlockSpec((r, length), lambda i: (0, 0))],
        out_specs=pl.BlockSpec((r, length), lambda i: (0, 0)),
        out_shape=jax.ShapeDtypeStruct((r, length), F32),
        compiler_params=_params("arbitrary"),
        name="fox_cumsum",
    )(x)


def _fox_kernel(q_ref, k_ref, v_ref, cq_ref, ck_ref, o_ref, m_ref, l_ref, acc_ref, *, hd):
    i = pl.program_id(2)
    tq = q_ref.shape[0]
    lane = lax.broadcasted_iota(jnp.int32, (1, LANES), 1)
    head_a = lane < hd
    q = q_ref[...]
    zero = jnp.zeros_like(q)
    qs = (jnp.where(head_a, q, zero), jnp.where(head_a, zero, q))
    cqs = (cq_ref[:, 0:1], cq_ref[:, 1:2])
    m_ref[...] = jnp.full(m_ref.shape, -jnp.inf, F32)
    l_ref[...] = jnp.zeros(l_ref.shape, F32)
    acc_ref[...] = jnp.zeros(acc_ref.shape, F32)
    row = lax.broadcasted_iota(jnp.int32, (tq, tq), 0)
    col = lax.broadcasted_iota(jnp.int32, (tq, tq), 1)

    def step(jk, diagonal):
        off = pl.multiple_of(jk * tq, tq)
        k = k_ref[pl.ds(off, tq), :]
        v = v_ref[pl.ds(off, tq), :]
        ck = ck_ref[:, pl.ds(off, tq)]
        for h in range(2):
            s = lax.dot_general(qs[h], k, _NT, preferred_element_type=F32)
            s = s + cqs[h] - ck[h:h + 1, :]
            if diagonal:
                s = jnp.where(col <= row, s, -jnp.inf)
            m_prev = m_ref[h]
            m_new = jnp.maximum(m_prev, jnp.max(s, axis=-1, keepdims=True))
            alpha = jnp.exp(m_prev - m_new)
            p = jnp.exp(s - m_new)
            l_ref[h] = alpha * l_ref[h] + jnp.sum(p, axis=-1, keepdims=True)
            acc_ref[h] = alpha * acc_ref[h] + jnp.dot(p.astype(BF16), v, preferred_element_type=F32)
            m_ref[h] = m_new

    def body(jk, carry):
        step(jk, False)
        return carry

    lax.fori_loop(0, i, body, 0)
    step(i, True)
    out = jnp.where(head_a, acc_ref[0] / l_ref[0], acc_ref[1] / l_ref[1])
    o_ref[...] = out.astype(BF16)


def _fox_prompt(qb, kb, vb, c_col, c_row, nb, seq, hd):
    n, fw = qb.shape
    pairs = fw // LANES
    tq = _tile(seq, TQ_FOX)
    nq = seq // tq
    qmap = lambda b, p, i: (b * nq + i, p)
    kmap = lambda b, p, i: (b, p)
    return pl.pallas_call(
        functools.partial(_fox_kernel, hd=hd),
        grid=(nb, pairs, nq),
        in_specs=[pl.BlockSpec((tq, LANES), qmap),
                  pl.BlockSpec((seq, LANES), kmap),
                  pl.BlockSpec((seq, LANES), kmap),
                  pl.BlockSpec((None, None, tq, 2), lambda b, p, i: (b, p, i, 0)),
                  pl.BlockSpec((None, None, 2, seq), lambda b, p, i: (b, p, 0, 0))],
        out_specs=pl.BlockSpec((tq, LANES), qmap),
        out_shape=jax.ShapeDtypeStruct((n, fw), BF16),
        scratch_shapes=[pltpu.VMEM((2, tq, 1), F32), pltpu.VMEM((2, tq, 1), F32),
                        pltpu.VMEM((2, tq, LANES), F32)],
        compiler_params=_params("arbitrary", "arbitrary", "arbitrary"),
        name="fox_attn",
    )(qb, kb, vb, c_col, c_row)


def _fox_cached_kernel(q_ref, kn_ref, vn_ref, kc_ref, vc_ref, lcn_ref, lct_ref, lnn_ref, lnt_ref, o_ref,
                       ck_ref, *, hd):
    ln, fw = q_ref.shape
    past = kc_ref.shape[0]
    heads = fw // hd
    lane8 = lax.broadcasted_iota(jnp.int32, (heads, LANES), 1)

    def body(i, carry):
        off = pl.multiple_of(i * LANES, LANES)
        x = _lane_cumsum(lct_ref[:, pl.ds(off, LANES)], lane8) + carry
        ck_ref[:, pl.ds(off, LANES)] = x
        return x[:, LANES - 1:LANES]

    c_last = lax.fori_loop(0, past // LANES, body, jnp.zeros((heads, 1), F32))
    r_i = lax.broadcasted_iota(jnp.int32, (ln, ln), 0)
    c_i = lax.broadcasted_iota(jnp.int32, (ln, ln), 1)
    causal = c_i <= r_i
    tril = causal.astype(F32)
    triu = (r_i <= c_i).astype(F32)
    ck_new = c_last + jnp.dot(lnt_ref[...], triu, precision=HIGHEST, preferred_element_type=F32)
    cache_total = jnp.dot(jnp.ones((ln, past), F32), lcn_ref[...], precision=HIGHEST,
                          preferred_element_type=F32)
    cq = cache_total + jnp.dot(tril, lnn_ref[...], precision=HIGHEST, preferred_element_type=F32)

    lane = lax.broadcasted_iota(jnp.int32, (1, LANES), 1)
    head_a = lane < hd
    for p in range(fw // LANES):
        cols = slice(p * LANES, (p + 1) * LANES)
        q = q_ref[:, cols]
        zero = jnp.zeros_like(q)
        kc = kc_ref[:, cols].astype(BF16)
        vc = vc_ref[:, cols].astype(BF16)
        kn = kn_ref[:, cols]
        vn = vn_ref[:, cols]
        outs = []
        for hh in range(2):
            h = 2 * p + hh
            qh = jnp.where(head_a, q, zero) if hh == 0 else jnp.where(head_a, zero, q)
            cqh = cq[:, h:h + 1]
            s_c = lax.dot_general(qh, kc, _NT, preferred_element_type=F32) + cqh - ck_ref[h:h + 1, :]
            s_n = lax.dot_general(qh, kn, _NT, preferred_element_type=F32) + cqh - ck_new[h:h + 1, :]
            s_n = jnp.where(causal, s_n, -jnp.inf)
            m = jnp.maximum(jnp.max(s_c, axis=-1, keepdims=True), jnp.max(s_n, axis=-1, keepdims=True))
            p_c = jnp.exp(s_c - m)
            p_n = jnp.exp(s_n - m)
            den = jnp.sum(p_c, axis=-1, keepdims=True) + jnp.sum(p_n, axis=-1, keepdims=True)
            acc = (jnp.dot(p_c.astype(BF16), vc, preferred_element_type=F32)
                   + jnp.dot(p_n.astype(BF16), vn, preferred_element_type=F32))
            outs.append(acc / den)
        o_ref[:, cols] = jnp.where(head_a, outs[0], outs[1]).astype(BF16)


def _fox_cached(l, qb, kb, vb, cache_k, cache_v, lf_cache_nat, lf_cache_t, lf_new_nat, lf_new_t, nb, ln, hd):
    n, fw = qb.shape
    past = cache_k.shape[2]
    heads = fw // hd
    rows = lambda b: (b, 0)
    cache = lambda b: (l, b, 0, 0)
    act = pl.BlockSpec((ln, fw), rows)
    return pl.pallas_call(
        functools.partial(_fox_cached_kernel, hd=hd),
        grid=(nb,),
        in_specs=[act, act, act,
                  pl.BlockSpec((None, None, past, fw), cache),
                  pl.BlockSpec((None, None, past, fw), cache),
                  pl.BlockSpec((None, None, past, heads), cache),
                  pl.BlockSpec((None, None, heads, past), cache),
                  pl.BlockSpec((None, ln, heads), lambda b: (b, 0, 0)),
                  pl.BlockSpec((None, heads, ln), lambda b: (b, 0, 0))],
        out_specs=act,
        out_shape=jax.ShapeDtypeStruct((n, fw), BF16),
        scratch_shapes=[pltpu.VMEM((heads, past), F32)],
        compiler_params=_params("arbitrary"),
        name="fox_attn_cached",
    )(qb, kb, vb, cache_k, cache_v, lf_cache_nat, lf_cache_t, lf_new_nat, lf_new_t)


def _gla_kernel(q_ref, k_ref, v_ref, g_ref, la_ref, s0_ref, gn_ref, y_ref, sout_ref, st_ref,
                *, chunk, nh, dk, dv):
    j = pl.program_id(1)
    t_rows = q_ref.shape[0]

    @pl.when(j == 0)
    def _():
        for h in range(nh):
            st_ref[h] = s0_ref[h].T

    row = lax.broadcasted_iota(jnp.int32, (chunk, chunk), 0)
    col = lax.broadcasted_iota(jnp.int32, (chunk, chunk), 1)
    causal = col <= row
    tril = causal.astype(F32)
    mid = chunk // 2
    for c in range(t_rows // chunk):
        rows = slice(c * chunk, (c + 1) * chunk)
        for h in range(nh):
            kc = slice(h * dk, (h + 1) * dk)
            vc = slice(h * dv, (h + 1) * dv)
            b = jnp.dot(tril, la_ref[rows, kc], precision=HIGHEST, preferred_element_type=F32)
            b_last = b[chunk - 1:chunk, :]
            b_mid = b[mid:mid + 1, :]
            q = q_ref[rows, kc].astype(F32)
            k = k_ref[rows, kc].astype(F32)
            v = v_ref[rows, vc]
            qi = (q * jnp.exp(b - b_mid)).astype(BF16)
            ki = (k * jnp.exp(b_mid - b)).astype(BF16)
            a = lax.dot_general(qi, ki, _NT, preferred_element_type=F32)
            a = jnp.where(causal, a, 0.0)
            o = jnp.dot(a.astype(BF16), v, preferred_element_type=F32)
            st = st_ref[h]
            qe = (q * jnp.exp(b)).astype(BF16)
            o = o + lax.dot_general(qe, st.astype(BF16), _NT, preferred_element_type=F32)
            kd = (k * jnp.exp(b_last - b)).astype(BF16)
            st_ref[h] = st * jnp.exp(b_last) + lax.dot_general(v, kd, _TN, preferred_element_type=F32)
            ms = jnp.mean(o * o, axis=-1, keepdims=True)
            y = o * lax.rsqrt(ms + LN_EPS) * gn_ref[...]
            y_ref[rows, vc] = (y * g_ref[rows, vc].astype(F32)).astype(BF16)

    @pl.when(j == pl.num_programs(1) - 1)
    def _():
        for h in range(nh):
            sout_ref[h] = st_ref[h].T


def _gla(l, gq, gk, gv, gg, la, s0, gn, nb, seq, s0_layered):
    n, kw = gq.shape
    nh, dk, dv = s0.shape[-3:]
    t_rows = _tile(seq, T_GLA)
    chunk = _tile(t_rows, GLA_CHUNK)
    nt = seq // t_rows
    rows = lambda b, j: (b * nt + j, 0)
    act = pl.BlockSpec((t_rows, kw), rows)
    if s0_layered:
        s0_spec = pl.BlockSpec((None, None, nh, dk, dv), lambda b, j: (l, b, 0, 0, 0))
    else:
        s0_spec = pl.BlockSpec((None, nh, dk, dv), lambda b, j: (b, 0, 0, 0))
    return pl.pallas_call(
        functools.partial(_gla_kernel, chunk=chunk, nh=nh, dk=dk, dv=dv),
        grid=(nb, nt),
        in_specs=[act, act, act, act, act, s0_spec,
                  _resident((None, 1, dv), lambda b, j: (l, 0, 0))],
        out_specs=[act, pl.BlockSpec((None, nh, dk, dv), lambda b, j: (b, 0, 0, 0))],
        out_shape=[jax.ShapeDtypeStruct((n, kw), BF16), jax.ShapeDtypeStruct((nb, nh, dk, dv), F32)],
        scratch_shapes=[pltpu.VMEM((nh, dv, dk), F32)],
        compiler_params=_params("arbitrary", "arbitrary"),
        name="gla",
    )(gq, gk, gv, gg, la, s0, gn)


def _merge_kernel(x_ref, mod_ref, yf_ref, ys_ref, yg_ref, wm_ref, wb_ref, wo_ref, g_ref, b_ref, o_ref,
                  *, d, alpha):
    x = x_ref[...]
    hb = (_ln(x) * (1.0 + mod_ref[:, d:2 * d]) + mod_ref[:, 0:d]).astype(BF16)
    merged = None
    for i, y_ref in enumerate((yf_ref, ys_ref, yg_ref)):
        gate = jax.nn.sigmoid(jnp.dot(hb, wm_ref[:, i * d:(i + 1) * d], preferred_element_type=F32))
        term = gate * jnp.dot(y_ref[...], wb_ref[i], preferred_element_type=F32)
        merged = term if merged is None else merged + term
    mix = jnp.dot(merged.astype(BF16), wo_ref[...], preferred_element_type=F32)
    z = alpha * x + mod_ref[:, 2 * d:3 * d] * mix
    o_ref[...] = _ln(z) * g_ref[...] + b_ref[...]


def _merge(l, x2d, mod, y_fox, y_sc, y_gla, wp, nb, seq, alpha):
    n, d = x2d.shape
    bw = y_fox.shape[1]
    tm = _tile(seq, TM_MERGE)
    nt = seq // tm
    rows = lambda b, j: (b * nt + j, 0)
    wl3 = lambda b, j: (l, 0, 0)
    yspec = pl.BlockSpec((tm, bw), rows)
    return pl.pallas_call(
        functools.partial(_merge_kernel, d=d, alpha=alpha),
        grid=(nb, nt),
        in_specs=[pl.BlockSpec((tm, d), rows),
                  pl.BlockSpec((None, 1, 6 * d), lambda b, j: (b, 0, 0)),
                  yspec, yspec, yspec,
                  _resident((None, d, 3 * d), wl3),
                  _resident((None, 3, bw, d), lambda b, j: (l, 0, 0, 0)),
                  _resident((None, d, d), wl3),
                  _resident((None, 1, d), wl3),
                  _resident((None, 1, d), wl3)],
        out_specs=pl.BlockSpec((tm, d), rows),
        out_shape=jax.ShapeDtypeStruct((n, d), F32),
        compiler_params=_params("arbitrary", "arbitrary"),
        name="merge_ln1",
    )(x2d, mod, y_fox, y_sc, y_gla, wp["w_merge"], wp["w_branch"], wp["w_out"], wp["ln1_g"], wp["ln1_b"])


def _ffn_kernel(x_ref, mod_ref, wu_ref, cw_ref, cb_ref, wd_ref, g_ref, b_ref, prev_ref, o_ref, st_ref,
                carry_ref, *, d, dff, chunk, alpha):
    j = pl.program_id(1)
    tm = x_ref.shape[0]
    first = j == 0
    x = x_ref[...]
    hb = (_ln(x) * (1.0 + mod_ref[:, 4 * d:5 * d]) + mod_ref[:, 3 * d:4 * d]).astype(BF16)
    f = jnp.zeros((tm, d), F32)
    for c0 in range(0, dff, chunk):
        cols = slice(c0, c0 + chunk)
        ug = jnp.dot(hb, wu_ref[:, cols], preferred_element_type=F32)
        uv = jnp.dot(hb, wu_ref[:, dff + c0:dff + c0 + chunk], preferred_element_type=F32)
        p0 = jnp.where(first, prev_ref[0:1, cols], carry_ref[SUBLANES - 2:SUBLANES - 1, cols])
        p1 = jnp.where(first, prev_ref[1:2, cols], carry_ref[SUBLANES - 1:SUBLANES, cols])
        ugc = _causal_conv3(ug, p0, p1, cw_ref, cb_ref, cols)
        carry_ref[:, cols] = ug[tm - SUBLANES:tm, :]
        a = (jax.nn.gelu(ugc) * uv).astype(BF16)
        f = f + jnp.dot(a, wd_ref[cols, :], preferred_element_type=F32)
    st_ref[...] = carry_ref[SUBLANES - 2:SUBLANES, :]
    z = alpha * x + mod_ref[:, 5 * d:6 * d] * f
    o_ref[...] = _ln(z) * g_ref[...] + b_ref[...]


def _ffn(l, x2d, mod, wp, prev, nb, seq, alpha):
    n, d = x2d.shape
    dff = wp["w_down"].shape[1]
    tm = _tile(seq, TM_FFN)
    nt = seq // tm
    chunk = _tile(dff, FFN_CHUNK)
    rows = lambda b, j: (b * nt + j, 0)
    wl3 = lambda b, j: (l, 0, 0)
    st = pl.BlockSpec((None, 2, dff), lambda b, j: (b, 0, 0))
    return pl.pallas_call(
        functools.partial(_ffn_kernel, d=d, dff=dff, chunk=chunk, alpha=alpha),
        grid=(nb, nt),
        in_specs=[pl.BlockSpec((tm, d), rows),
                  pl.BlockSpec((None, 1, 6 * d), lambda b, j: (b, 0, 0)),
                  _resident((None, d, 2 * dff), wl3),
                  _resident((None, 3, dff), wl3),
                  _resident((None, 1, dff), wl3),
                  _resident((None, dff, d), wl3),
                  _resident((None, 1, d), wl3),
                  _resident((None, 1, d), wl3),
                  st],
        out_specs=[pl.BlockSpec((tm, d), rows), st],
        out_shape=[jax.ShapeDtypeStruct((n, d), F32), jax.ShapeDtypeStruct((nb, 2, dff), F32)],
        scratch_shapes=[pltpu.VMEM((SUBLANES, dff), F32)],
        compiler_params=_params("arbitrary", "arbitrary"),
        name="conv_ffn_ln2",
    )(x2d, mod, wp["w_up"], wp["ffn_conv_w"], wp["ffn_conv_b"], wp["w_down"], wp["ln2_g"], wp["ln2_b"], prev)


def _pack_weights(w_in, b_fox_f, w_gla_up, b_gla_a, gla_norm_g, sc_conv_w, sc_conv_b, w_branch, w_out,
                  ln1_g, ln1_b, w_up, ffn_conv_w, ffn_conv_b, w_down, ln2_g, ln2_b, heads):
    depth, d, _ = w_in.shape
    fw = sc_conv_w.shape[-1]
    rank = w_gla_up.shape[1]
    o = 0
    fox = w_in[:, :, o:o + 3 * fw]; o += 3 * fw
    wff = w_in[:, :, o:o + heads]; o += heads
    sc = w_in[:, :, o:o + 3 * fw]; o += 3 * fw
    gla = w_in[:, :, o:o + 4 * fw]; o += 4 * fw
    glr = w_in[:, :, o:o + rank]; o += rank
    wm = w_in[:, :, o:]
    row = lambda a: a.reshape(depth, 1, a.shape[-1])
    return {
        "fw": fw,
        "w_main": jnp.concatenate([fox, sc, gla], axis=-1).astype(BF16),
        "w_fft": jnp.swapaxes(wff, 1, 2).astype(BF16),
        "w_glr": jnp.pad(glr, ((0, 0), (0, 0), (0, LANES - rank))).astype(BF16),
        "b_fox": b_fox_f.reshape(depth, heads, 1),
        "w_gla_up": jnp.pad(w_gla_up, ((0, 0), (0, LANES - rank), (0, 0))).astype(BF16),
        "b_gla": row(b_gla_a),
        "gla_norm_g": row(gla_norm_g),
        "sc_w": sc_conv_w, "sc_b": row(sc_conv_b),
        "w_merge": wm.astype(BF16),
        "w_branch": w_branch.astype(BF16),
        "w_out": w_out.astype(BF16),
        "ln1_g": row(ln1_g), "ln1_b": row(ln1_b),
        "w_up": w_up.astype(BF16),
        "ffn_conv_w": ffn_conv_w, "ffn_conv_b": row(ffn_conv_b),
        "w_down": w_down.astype(BF16),
        "ln2_g": row(ln2_g), "ln2_b": row(ln2_b),
    }


def _layer(l, x2d, mod, wp, nb, seq, hd, alpha, fox_cache, sc_prev, gla_s0, ffn_prev):
    fw = wp["fw"]
    heads = fw // hd
    dk = gla_s0.shape[-2]
    (qb, kf, vf, kb, vb, lft, y_sc, sc_new, gq, gk, gv, gg, la) = _inproj(l, x2d, mod, wp, sc_prev, nb, seq, hd, dk)
    lf_t = lft.reshape(nb, -1, heads, lft.shape[-1]).transpose(2, 0, 1, 3).reshape(heads, nb, seq)
    logf = lf_t.transpose(1, 2, 0)
    if fox_cache is None:
        c = _cumsum_rows(lf_t.reshape(heads * nb, seq)).reshape(heads // 2, 2, nb, seq)
        y_fox = _fox_prompt(qb, kb, vb, c.transpose(2, 0, 3, 1), c.transpose(2, 0, 1, 3), nb, seq, hd)
        y_gla, gla_new = _gla(l, gq, gk, gv, gg, la, gla_s0, wp["gla_norm_g"], nb, seq, False)
    else:
        cache_k, cache_v, lf_cache_nat, lf_cache_t = fox_cache
        y_fox = _fox_cached(l, qb, kb, vb, cache_k, cache_v, lf_cache_nat, lf_cache_t,
                            logf, lf_t.transpose(1, 0, 2), nb, seq, hd)
        y_gla, gla_new = _gla(l, gq, gk, gv, gg, la, gla_s0, wp["gla_norm_g"], nb, seq, True)
    x1 = _merge(l, x2d, mod, y_fox, y_sc, y_gla, wp, nb, seq, alpha)
    x2, ffn_new = _ffn(l, x1, mod, wp, ffn_prev, nb, seq, alpha)
    k_out = kf.reshape(nb, seq, heads, hd)
    v_out = vf.reshape(nb, seq, heads, hd)
    return x2, (k_out, v_out, logf, sc_new, gla_new, ffn_new)


def kernel(x_prompt, x_sample, c_prompt, c_sample, cache_fox_k, cache_fox_v, cache_fox_logf, state_shortconv, state_gla, state_ffn_conv, w_ada, b_ada, w_in, b_fox_f, w_gla_up, b_gla_a, gla_norm_g, sc_conv_w, sc_conv_b, w_branch, w_out, ln1_g, ln1_b, w_up, ffn_conv_w, ffn_conv_b, w_down, ln2_g, ln2_b):
    bp, sp, d = x_prompt.shape
    bs, ss, _ = x_sample.shape
    depth = w_in.shape[0]
    _, _, past, heads, hd = cache_fox_k.shape
    fw = heads * hd
    dff = ffn_conv_w.shape[-1]
    assert sc_conv_w.shape[-1] == fw and state_gla.shape[2] * state_gla.shape[3] == fw
    assert state_gla.shape[2] * state_gla.shape[4] == fw and fw % LANES == 0 and 2 * hd == LANES
    assert sc_conv_w.shape[1] == 3 and ffn_conv_w.shape[1] == 3
    alpha = (2.0 * depth) ** 0.25

    wp = _pack_weights(w_in, b_fox_f, w_gla_up, b_gla_a, gla_norm_g, sc_conv_w, sc_conv_b, w_branch, w_out,
                       ln1_g, ln1_b, w_up, ffn_conv_w, ffn_conv_b, w_down, ln2_g, ln2_b, heads)
    mods = _ada(jnp.concatenate([c_prompt, c_sample], axis=0), w_ada, b_ada)
    cache_k = cache_fox_k.reshape(depth, bs, past, fw)
    cache_v = cache_fox_v.reshape(depth, bs, past, fw)
    lf_cache_t = jnp.swapaxes(cache_fox_logf, 2, 3)

    xp = x_prompt.reshape(bp * sp, d)
    xs = x_sample.reshape(bs * ss, d)
    zeros_sc = jnp.zeros((bp, 2, fw), F32)
    zeros_gla = jnp.zeros((bp,) + state_gla.shape[2:], F32)
    zeros_ffn = jnp.zeros((bp, 2, dff), F32)
    outs_p = [[] for _ in range(6)]
    outs_s = [[] for _ in range(6)]
    for l in range(depth):
        mod_p = mods[l, :bp].reshape(bp, 1, 6 * d)
        mod_s = mods[l, bp:].reshape(bs, 1, 6 * d)
        xp, st_p = _layer(l, xp, mod_p, wp, bp, sp, hd, alpha, None, zeros_sc, zeros_gla, zeros_ffn)
        xs, st_s = _layer(l, xs, mod_s, wp, bs, ss, hd, alpha,
                          (cache_k, cache_v, cache_fox_logf, lf_cache_t),
                          state_shortconv[l], state_gla, state_ffn_conv[l])
        for i in range(6):
            outs_p[i].append(st_p[i])
            outs_s[i].append(st_s[i])
    stacked_p = [jnp.stack(o) for o in outs_p]
    stacked_s = [jnp.stack(o) for o in outs_s]
    return (xp.reshape(bp, sp, d), xs.reshape(bs, ss, d), *stacked_p, *stacked_s)
```

```python
import functools

import jax
import jax.numpy as jnp
from jax import lax
from jax.experimental import pallas as pl
from jax.experimental.pallas import tpu as pltpu

F32 = jnp.float32
BF16 = jnp.bfloat16
LN_EPS = 1e-5
GLA_TAU = 16.0
LANES = 128
SUBLANES = 8
VMEM_LIMIT = 56 * 1024 * 1024
HIGHEST = lax.Precision.HIGHEST
LOG2E = 1.4426950408889634
FOX_UNDERFLOW = 160.0
FOX_NORM_MARGIN = 1.01

TM_PROJ = 512
TM_MERGE = 512
TM_FFN = 512
TQ_FOX = 512
T_GLA = 512
GLA_CHUNK = 128
FFN_CHUNK = 256
ADA_TN = 1024

_NT = (((1,), (1,)), ((), ()))
_TN = (((0,), (0,)), ((), ()))


def _tile(n, pref):
    if n <= pref:
        return n
    t = pref
    while n % t:
        t //= 2
    return t


def _params(*sem):
    return pltpu.CompilerParams(dimension_semantics=sem, vmem_limit_bytes=VMEM_LIMIT)


def _resident(block, index_map):
    return pl.BlockSpec(block, index_map, pipeline_mode=pl.Buffered(1))


def _log_sigmoid(x):
    return jnp.minimum(x, 0.0) - jnp.log1p(jnp.exp(-jnp.abs(x)))


def _ln(x):
    mu = jnp.mean(x, axis=-1, keepdims=True)
    xc = x - mu
    var = jnp.mean(xc * xc, axis=-1, keepdims=True)
    return xc * lax.rsqrt(var + LN_EPS)


def _causal_conv3(u, p0, p1, w_ref, b_ref, cols):
    tm = u.shape[0]
    row = lax.broadcasted_iota(jnp.int32, (tm, 1), 0)
    u1 = jnp.where(row == 0, p1, pltpu.roll(u, 1, 0))
    u2 = jnp.where(row == 0, p0, jnp.where(row == 1, p1, pltpu.roll(u, 2, 0)))
    y = b_ref[:, cols] + u2 * w_ref[0:1, cols]
    y = y + u1 * w_ref[1:2, cols]
    return y + u * w_ref[2:3, cols]


def _ada_kernel(c_ref, w_ref, b_ref, o_ref):
    c = c_ref[...]
    s = (c * jax.nn.sigmoid(c)).astype(BF16)
    o_ref[...] = jnp.dot(s, w_ref[...].astype(BF16), preferred_element_type=F32) + b_ref[...]


def _ada(c_all, w_ada, b_ada):
    depth, d, n6 = w_ada.shape
    r = c_all.shape[0]
    tn = _tile(n6, ADA_TN)
    return pl.pallas_call(
        _ada_kernel,
        grid=(depth, n6 // tn),
        in_specs=[pl.BlockSpec((r, d), lambda l, n: (0, 0)),
                  pl.BlockSpec((None, d, tn), lambda l, n: (l, 0, n)),
                  pl.BlockSpec((None, 1, tn), lambda l, n: (l, 0, n))],
        out_specs=pl.BlockSpec((None, r, tn), lambda l, n: (l, 0, n)),
        out_shape=jax.ShapeDtypeStruct((depth, r, n6), F32),
        compiler_params=_params("arbitrary", "arbitrary"),
        name="ada_mod",
    )(c_all, w_ada, b_ada.reshape(depth, 1, n6))


def _inproj_kernel(x_ref, mod_ref, w_ref, wff_ref, wglr_ref, bff_ref, wup_ref, ba_ref, scw_ref, scb_ref,
                   prev_ref, kf_all_ref, vf_all_ref,
                   q_ref, kf_ref, vf_ref, kb_ref, vb_ref, lf_ref, ysc_ref, scst_ref,
                   gq_ref, gk_ref, gv_ref, gg_ref, la_ref,
                   carry_ref, *, d, fw, q_scale, gq_scale, transposed):
    del kf_all_ref, vf_all_ref
    j = pl.program_id(1)
    tm = x_ref.shape[0]
    h = _ln(x_ref[...]) * (1.0 + mod_ref[:, d:2 * d]) + mod_ref[:, 0:d]
    hb = h.astype(BF16)

    def proj(g):
        return jnp.dot(hb, w_ref[:, g * fw:(g + 1) * fw], preferred_element_type=F32)

    q = proj(0) * q_scale
    k = proj(1)
    v = proj(2)
    kf_ref[...] = k
    vf_ref[...] = v
    kb_ref[...] = k.astype(BF16)
    if transposed:
        q_ref[...] = q.T.astype(BF16)
        vb_ref[...] = v.T.astype(BF16)
    else:
        q_ref[...] = q.astype(BF16)
        vb_ref[...] = v.astype(BF16)
    fft =lax.dot_general(wff_ref[...], hb, _NT, preferred_element_type=F32)
    lf_ref[...] = _log_sigmoid(fft + bff_ref[...])

    first = j == 0
    p0 = jnp.where(first, prev_ref[0:1, :], carry_ref[SUBLANES - 2:SUBLANES - 1, :])
    p1 = jnp.where(first, prev_ref[1:2, :], carry_ref[SUBLANES - 1:SUBLANES, :])
    sb = proj(3)
    u = proj(4) * proj(5)
    y = _causal_conv3(u, p0, p1, scw_ref, scb_ref, slice(None))
    ysc_ref[...] = (sb * y).astype(BF16)
    carry_ref[...] = u[tm - SUBLANES:tm, :]
    scst_ref[...] = carry_ref[SUBLANES - 2:SUBLANES, :]

    gq_ref[...] = (proj(6) * gq_scale).astype(BF16)
    gk_ref[...] = proj(7).astype(BF16)
    gv_ref[...] = proj(8).astype(BF16)
    gg = proj(9)
    gg_ref[...] = (gg * jax.nn.sigmoid(gg)).astype(BF16)
    glr = jnp.dot(hb, wglr_ref[...], preferred_element_type=F32)
    lap = jnp.dot(glr.astype(BF16), wup_ref[...], preferred_element_type=F32) + ba_ref[...]
    la_ref[...] = _log_sigmoid(lap) / GLA_TAU


def _inproj(l, x2d, mod, wp, sc_prev, kf_all, vf_all, nb, seq, hd, dk, transposed):
    n, d = x2d.shape
    fw = wp["fw"]
    heads = fw // hd
    tm = _tile(seq, TM_PROJ)
    nt = seq // tm
    rows = lambda b, j: (b * nt + j, 0)
    wl = lambda b, j: (l, 0, 0)
    act = lambda dt: jax.ShapeDtypeStruct((n, fw), dt)
    act_spec = pl.BlockSpec((tm, fw), rows)
    if transposed:
        t_shape, t_spec = jax.ShapeDtypeStruct((fw, n), BF16), pl.BlockSpec((fw, tm), lambda b, j: (0, b * nt + j))
        q_scale = LOG2E * hd ** -0.5
    else:
        t_shape, t_spec, q_scale = act(BF16), act_spec, hd ** -0.5
    stacked = pl.BlockSpec((None, tm, fw), lambda b, j: (l, b * nt + j, 0))
    hbm = pl.BlockSpec(memory_space=pl.ANY)
    kern = functools.partial(_inproj_kernel, d=d, fw=fw, q_scale=q_scale, gq_scale=dk ** -0.5,
                             transposed=transposed)
    return pl.pallas_call(
        kern,
        grid=(nb, nt),
        in_specs=[pl.BlockSpec((tm, d), rows),
                  pl.BlockSpec((None, 1, 6 * d), lambda b, j: (b, 0, 0)),
                  _resident((None, d, 10 * fw), wl),
                  _resident((None, heads, d), wl),
                  _resident((None, d, LANES), wl),
                  _resident((None, heads, 1), wl),
                  _resident((None, LANES, fw), wl),
                  _resident((None, 1, fw), wl),
                  _resident((None, 3, fw), wl),
                  _resident((None, 1, fw), wl),
                  pl.BlockSpec((None, 2, fw), lambda b, j: (b, 0, 0)),
                  hbm, hbm],
        out_specs=[t_spec, stacked, stacked, act_spec, t_spec,
                   pl.BlockSpec((None, heads, tm), lambda b, j: (b * nt + j, 0, 0)),
                   act_spec,
                   pl.BlockSpec((None, 2, fw), lambda b, j: (b, 0, 0)),
                   act_spec, act_spec, act_spec, act_spec, act_spec],
        out_shape=[t_shape, jax.ShapeDtypeStruct(kf_all.shape, F32), jax.ShapeDtypeStruct(vf_all.shape, F32),
                   act(BF16), t_shape,
                   jax.ShapeDtypeStruct((nb * nt, heads, tm), F32),
                   act(BF16),
                   jax.ShapeDtypeStruct((nb, 2, fw), F32),
                   act(BF16), act(BF16), act(BF16), act(BF16), act(F32)],
        input_output_aliases={11: 1, 12: 2},
        scratch_shapes=[pltpu.VMEM((SUBLANES, fw), F32)],
        compiler_params=_params("arbitrary", "arbitrary"),
        name="in_proj",
    )(x2d, mod, wp["w_main"], wp["w_fft"], wp["w_glr"], wp["b_fox"], wp["w_gla_up"], wp["b_gla"],
      wp["sc_w"], wp["sc_b"], sc_prev, kf_all, vf_all)


def _lane_cumsum(x, lane):
    for k in range(7):
        s = 1 << k
        x = x + jnp.where(lane >= s, pltpu.roll(x, s, 1), 0.0)
    return x


def _cumsum_kernel(x_ref, cp_ref, c1_ref, c2_ref, c3_ref):
    r, length = x_ref.shape
    lane = lax.broadcasted_iota(jnp.int32, (r, LANES), 1)

    def body(i, carry):
        off = pl.multiple_of(i * LANES, LANES)
        x = _lane_cumsum(x_ref[:, pl.ds(off, LANES)], lane) + carry
        c = x * LOG2E
        c1 = c.astype(BF16)
        r1 = c - c1.astype(F32)
        c2 = r1.astype(BF16)
        c3 = (r1 - c2.astype(F32)).astype(BF16)
        cp_ref[:, pl.ds(off, LANES)] = c
        c1_ref[:, pl.ds(off, LANES)] = c1
        c2_ref[:, pl.ds(off, LANES)] = c2
        c3_ref[:, pl.ds(off, LANES)] = c3
        return x[:, LANES - 1:LANES]

    lax.fori_loop(0, length // LANES, body, jnp.zeros((r, 1), F32))


def _cumsum_rows(x):
    r, length = x.shape
    spec = pl.BlockSpec((r, length), lambda i: (0, 0))
    return pl.pallas_call(
        _cumsum_kernel,
        grid=(1,),
        in_specs=[spec],
        out_specs=[spec, spec, spec, spec],
        out_shape=[jax.ShapeDtypeStruct((r, length), F32)] + [jax.ShapeDtypeStruct((r, length), BF16)] * 3,
        compiler_params=_params("arbitrary"),
        name="fox_cumsum",
    )(x)


def _fox_kernel(cend_ref, qt_ref, aqt_ref, k_ref, ak_ref, vt_ref, o_ref, m_ref, acc_ref, qs_ref, kmax_ref,
                *, hd):
    b = pl.program_id(0)
    pr = pl.program_id(1)
    i = pl.program_id(2)
    tq = qt_ref.shape[1]
    nk = k_ref.shape[0] // tq
    lane = lax.broadcasted_iota(jnp.int32, (1, LANES), 1)
    head_a = lane < hd

    @pl.when(i == 0)
    def _():
        kk = k_ref[...].astype(F32)
        kk = kk * kk
        for h in range(2):
            sel = head_a if h == 0 else jnp.logical_not(head_a)
            n2 = jnp.sum(jnp.where(sel, kk, 0.0), axis=-1, keepdims=True)
            kmax_ref[h] = jnp.sqrt(jnp.max(n2))

    qs_ref[0, 0:hd, :] = qt_ref[0:hd, :]
    qs_ref[0, hd:, :] = aqt_ref[hd:, :]
    qs_ref[1, 0:hd, :] = aqt_ref[0:hd, :]
    qs_ref[1, hd:, :] = qt_ref[hd:, :]
    m_ref[...] = jnp.full(m_ref.shape, -jnp.inf, F32)
    acc_ref[...] = jnp.zeros(acc_ref.shape, F32)

    def step(jk, tk, diagonal):
        off = pl.multiple_of(jk * tq, tq)
        ones = jnp.ones((hd, tk), BF16)
        k = k_ref[pl.ds(off, tk), :]
        ak = ak_ref[pl.ds(off, tk), :]
        ks = (jnp.where(head_a, k, ak), jnp.where(head_a, ak, k))
        vts = (jnp.concatenate([vt_ref[0:hd, pl.ds(off, tk)], ones], axis=0),
               jnp.concatenate([ones, vt_ref[hd:, pl.ds(off, tk)]], axis=0))
        sts = [jnp.dot(ks[h], qs_ref[h], preferred_element_type=F32) for h in range(2)]
        if diagonal:
            krow = lax.broadcasted_iota(jnp.int32, (tk, tq), 0)
            qcol = lax.broadcasted_iota(jnp.int32, (tk, tq), 1)
            sts = [jnp.where(krow <= qcol, st, -jnp.inf) for st in sts]
        m_prevs = [m_ref[h] for h in range(2)]
        m_news = [jnp.maximum(m_prevs[h], jnp.max(sts[h], axis=0, keepdims=True)) for h in range(2)]
        for h in range(2):
            p = jnp.exp2(sts[h] - m_news[h]).astype(BF16)
            alpha = jnp.exp2(m_prevs[h] - m_news[h])
            acc_ref[h] = alpha * acc_ref[h] + jnp.dot(vts[h], p, preferred_element_type=F32)
            m_ref[h] = m_news[h]

    step(i, tq, True)

    base = ((b * pl.num_programs(1) + pr) * 2) * nk
    slack = []
    for h in range(2):
        qf = qt_ref[h * hd:(h + 1) * hd, :].astype(F32)
        qn = jnp.sqrt(jnp.max(jnp.sum(qf * qf, axis=0, keepdims=True)))
        slack.append(qn * kmax_ref[h] * FOX_NORM_MARGIN - jnp.min(m_ref[h]))
    before = jnp.maximum(i - 1, 0)

    def needed(j):
        r = False
        for h in range(2):
            gap = cend_ref[base + h * nk + before] - cend_ref[base + h * nk + j]
            r = jnp.logical_or(r, slack[h] + gap > -FOX_UNDERFLOW)
        return r

    j_lo = lax.while_loop(lambda j: jnp.logical_and(j > 0, needed(jnp.maximum(j - 1, 0))), lambda j: j - 1, i)
    n_off = i - j_lo

    def body(t, carry):
        step(j_lo + 2 * t, 2 * tq, False)
        return carry

    lax.fori_loop(0, n_off // 2, body, 0)

    @pl.when(n_off % 2 == 1)
    def _():
        step(i - 1, tq, False)

    acc_a = acc_ref[0]
    acc_b = acc_ref[1]
    out_t = jnp.concatenate([acc_a[0:hd] / acc_a[hd:hd + 1], acc_b[hd:] / acc_b[0:1]], axis=0)
    o_ref[...] = out_t.T.astype(BF16)


def _fox_prompt(cend, qt, aqt, kb, ak, vt, nb, seq, hd, tq):
    fw, n = qt.shape
    pairs = fw // LANES
    nq = seq // tq
    qmap = lambda b, p, i, c: (p, b * nq + i)
    kmap = lambda b, p, i, c: (b, p)
    grid_spec = pltpu.PrefetchScalarGridSpec(
        num_scalar_prefetch=1,
        grid=(nb, pairs, nq),
        in_specs=[pl.BlockSpec((LANES, tq), qmap), pl.BlockSpec((LANES, tq), qmap),
                  pl.BlockSpec((seq, LANES), kmap), pl.BlockSpec((seq, LANES), kmap),
                  pl.BlockSpec((LANES, seq), lambda b, p, i, c: (p, b))],
        out_specs=pl.BlockSpec((tq, LANES), lambda b, p, i, c: (b * nq + i, p)),
        scratch_shapes=[pltpu.VMEM((2, 1, tq), F32), pltpu.VMEM((2, LANES, tq), F32),
                        pltpu.VMEM((2, LANES, tq), BF16), pltpu.SMEM((2,), F32)],
    )
    return pl.pallas_call(
        functools.partial(_fox_kernel, hd=hd),
        grid_spec=grid_spec,
        out_shape=jax.ShapeDtypeStruct((n, fw), BF16),
        compiler_params=_params("arbitrary", "arbitrary", "arbitrary"),
        name="fox_attn",
    )(cend, qt, aqt, kb, ak, vt)


def _fox_bias_operands(cp, c1, c2, c3, nb, seq, hd, tq):
    heads = cp.shape[0] // nb
    n = nb * seq
    cq = jnp.concatenate([c.reshape(heads, 1, n) for c in (c1, c2, c3)], axis=1)
    ones = jnp.ones_like(cq)

    def place(a):
        a = a.reshape(heads // 2, 2, 6, n)
        z = jnp.zeros((heads // 2, hd - 6, n), BF16)
        return jnp.concatenate([a[:, 1], z, a[:, 0], z], axis=1).reshape(heads * hd, n)

    aq_t = place(jnp.concatenate([cq, ones], axis=1))
    ak = place(jnp.concatenate([ones, -cq], axis=1)).T
    cend = cp.reshape(heads // 2, 2, nb, seq)[..., tq - 1::tq].transpose(2, 0, 1, 3).reshape(-1)
    return aq_t, ak, cend


def _fox_cached_kernel(q_ref, kn_ref, vn_ref, kc_ref, vc_ref, lcn_ref, lct_ref, lnn_ref, lnt_ref, o_ref,
                       ck_ref, *, hd):
    ln, fw = q_ref.shape
    past = kc_ref.shape[0]
    heads = fw // hd
    lane8 = lax.broadcasted_iota(jnp.int32, (heads, LANES), 1)

    def body(i, carry):
        off = pl.multiple_of(i * LANES, LANES)
        x = _lane_cumsum(lct_ref[:, pl.ds(off, LANES)], lane8) + carry
        ck_ref[:, pl.ds(off, LANES)] = x
        return x[:, LANES - 1:LANES]

    c_last = lax.fori_loop(0, past // LANES, body, jnp.zeros((heads, 1), F32))
    r_i = lax.broadcasted_iota(jnp.int32, (ln, ln), 0)
    c_i = lax.broadcasted_iota(jnp.int32, (ln, ln), 1)
    causal = c_i <= r_i
    tril = causal.astype(F32)
    triu = (r_i <= c_i).astype(F32)
    ck_new = c_last + jnp.dot(lnt_ref[...], triu, precision=HIGHEST, preferred_element_type=F32)
    cache_total = jnp.dot(jnp.ones((ln, past), F32), lcn_ref[...], precision=HIGHEST,
                          preferred_element_type=F32)
    cq = cache_total + jnp.dot(tril, lnn_ref[...], precision=HIGHEST, preferred_element_type=F32)

    lane = lax.broadcasted_iota(jnp.int32, (1, LANES), 1)
    head_a = lane < hd
    for p in range(fw // LANES):
        cols = slice(p * LANES, (p + 1) * LANES)
        q = q_ref[:, cols]
        zero = jnp.zeros_like(q)
        kc = kc_ref[:, cols].astype(BF16)
        vc = vc_ref[:, cols].astype(BF16)
        kn = kn_ref[:, cols]
        vn = vn_ref[:, cols]
        outs = []
        for hh in range(2):
            h = 2 * p + hh
            qh = jnp.where(head_a, q, zero) if hh == 0 else jnp.where(head_a, zero, q)
            cqh = cq[:, h:h + 1]
            s_c = lax.dot_general(qh, kc, _NT, preferred_element_type=F32) + cqh - ck_ref[h:h + 1, :]
            s_n = lax.dot_general(qh, kn, _NT, preferred_element_type=F32) + cqh - ck_new[h:h + 1, :]
            s_n = jnp.where(causal, s_n, -jnp.inf)
            m = jnp.maximum(jnp.max(s_c, axis=-1, keepdims=True), jnp.max(s_n, axis=-1, keepdims=True))
            p_c = jnp.exp(s_c - m)
            p_n = jnp.exp(s_n - m)
            den = jnp.sum(p_c, axis=-1, keepdims=True) + jnp.sum(p_n, axis=-1, keepdims=True)
            acc = (jnp.dot(p_c.astype(BF16), vc, preferred_element_type=F32)
                   + jnp.dot(p_n.astype(BF16), vn, preferred_element_type=F32))
            outs.append(acc / den)
        o_ref[:, cols] = jnp.where(head_a, outs[0], outs[1]).astype(BF16)


def _fox_cached(l, qb, kb, vb, cache_k, cache_v, lf_cache_nat, lf_cache_t, lf_new_nat, lf_new_t, nb, ln, hd):
    n, fw = qb.shape
    past = cache_k.shape[2]
    heads = fw // hd
    rows = lambda b: (b, 0)
    cache = lambda b: (l, b, 0, 0)
    act = pl.BlockSpec((ln, fw), rows)
    return pl.pallas_call(
        functools.partial(_fox_cached_kernel, hd=hd),
        grid=(nb,),
        in_specs=[act, act, act,
                  pl.BlockSpec((None, None, past, fw), cache),
                  pl.BlockSpec((None, None, past, fw), cache),
                  pl.BlockSpec((None, None, past, heads), cache),
                  pl.BlockSpec((None, None, heads, past), cache),
                  pl.BlockSpec((None, ln, heads), lambda b: (b, 0, 0)),
                  pl.BlockSpec((None, heads, ln), lambda b: (b, 0, 0))],
        out_specs=act,
        out_shape=jax.ShapeDtypeStruct((n, fw), BF16),
        scratch_shapes=[pltpu.VMEM((heads, past), F32)],
        compiler_params=_params("arbitrary"),
        name="fox_attn_cached",
    )(qb, kb, vb, cache_k, cache_v, lf_cache_nat, lf_cache_t, lf_new_nat, lf_new_t)


def _gla_kernel(q_ref, k_ref, v_ref, g_ref, la_ref, s0_ref, gn_ref, y_ref, sout_ref, st_ref,
                *, chunk, nh, dk, dv):
    j = pl.program_id(1)
    t_rows = q_ref.shape[0]

    @pl.when(j == 0)
    def _():
        for h in range(nh):
            st_ref[h] = s0_ref[h].T

    row = lax.broadcasted_iota(jnp.int32, (chunk, chunk), 0)
    col = lax.broadcasted_iota(jnp.int32, (chunk, chunk), 1)
    causal = col <= row
    tril = causal.astype(F32)
    mid = chunk // 2
    for c in range(t_rows // chunk):
        rows = slice(c * chunk, (c + 1) * chunk)
        for h in range(nh):
            kc = slice(h * dk, (h + 1) * dk)
            vc = slice(h * dv, (h + 1) * dv)
            b = jnp.dot(tril, la_ref[rows, kc], precision=HIGHEST, preferred_element_type=F32)
            b_last = b[chunk - 1:chunk, :]
            b_mid = b[mid:mid + 1, :]
            q = q_ref[rows, kc].astype(F32)
            k = k_ref[rows, kc].astype(F32)
            v = v_ref[rows, vc]
            qi = (q * jnp.exp(b - b_mid)).astype(BF16)
            ki = (k * jnp.exp(b_mid - b)).astype(BF16)
            a = lax.dot_general(qi, ki, _NT, preferred_element_type=F32)
            a = jnp.where(causal, a, 0.0)
            o = jnp.dot(a.astype(BF16), v, preferred_element_type=F32)
            st = st_ref[h]
            qe = (q * jnp.exp(b)).astype(BF16)
            o = o + lax.dot_general(qe, st.astype(BF16), _NT, preferred_element_type=F32)
            kd = (k * jnp.exp(b_last - b)).astype(BF16)
            st_ref[h] = st * jnp.exp(b_last) + lax.dot_general(v, kd, _TN, preferred_element_type=F32)
            ms = jnp.mean(o * o, axis=-1, keepdims=True)
            y = o * lax.rsqrt(ms + LN_EPS) * gn_ref[...]
            y_ref[rows, vc] = (y * g_ref[rows, vc].astype(F32)).astype(BF16)

    @pl.when(j == pl.num_programs(1) - 1)
    def _():
        for h in range(nh):
            sout_ref[h] = st_ref[h].T


def _gla(l, gq, gk, gv, gg, la, s0, gn, nb, seq, s0_layered):
    n, kw = gq.shape
    nh, dk, dv = s0.shape[-3:]
    t_rows = _tile(seq, T_GLA)
    chunk = _tile(t_rows, GLA_CHUNK)
    nt = seq // t_rows
    rows = lambda b, j: (b * nt + j, 0)
    act = pl.BlockSpec((t_rows, kw), rows)
    if s0_layered:
        s0_spec = pl.BlockSpec((None, None, nh, dk, dv), lambda b, j: (l, b, 0, 0, 0))
    else:
        s0_spec = pl.BlockSpec((None, nh, dk, dv), lambda b, j: (b, 0, 0, 0))
    return pl.pallas_call(
        functools.partial(_gla_kernel, chunk=chunk, nh=nh, dk=dk, dv=dv),
        grid=(nb, nt),
        in_specs=[act, act, act, act, act, s0_spec,
                  _resident((None, 1, dv), lambda b, j: (l, 0, 0))],
        out_specs=[act, pl.BlockSpec((None, nh, dk, dv), lambda b, j: (b, 0, 0, 0))],
        out_shape=[jax.ShapeDtypeStruct((n, kw), BF16), jax.ShapeDtypeStruct((nb, nh, dk, dv), F32)],
        scratch_shapes=[pltpu.VMEM((nh, dv, dk), F32)],
        compiler_params=_params("arbitrary", "arbitrary"),
        name="gla",
    )(gq, gk, gv, gg, la, s0, gn)


def _merge_kernel(x_ref, mod_ref, yf_ref, ys_ref, yg_ref, wm_ref, wb_ref, wo_ref, g_ref, b_ref, o_ref,
                  *, d, alpha):
    x = x_ref[...]
    hb = (_ln(x) * (1.0 + mod_ref[:, d:2 * d]) + mod_ref[:, 0:d]).astype(BF16)
    merged = None
    for i, y_ref in enumerate((yf_ref, ys_ref, yg_ref)):
        gate = jax.nn.sigmoid(jnp.dot(hb, wm_ref[:, i * d:(i + 1) * d], preferred_element_type=F32))
        term = gate * jnp.dot(y_ref[...], wb_ref[i], preferred_element_type=F32)
        merged = term if merged is None else merged + term
    mix = jnp.dot(merged.astype(BF16), wo_ref[...], preferred_element_type=F32)
    z = alpha * x + mod_ref[:, 2 * d:3 * d] * mix
    o_ref[...] = _ln(z) * g_ref[...] + b_ref[...]


def _merge(l, x2d, mod, y_fox, y_sc, y_gla, wp, nb, seq, alpha):
    n, d = x2d.shape
    bw = y_fox.shape[1]
    tm = _tile(seq, TM_MERGE)
    nt = seq // tm
    rows = lambda b, j: (b * nt + j, 0)
    wl3 = lambda b, j: (l, 0, 0)
    yspec = pl.BlockSpec((tm, bw), rows)
    return pl.pallas_call(
        functools.partial(_merge_kernel, d=d, alpha=alpha),
        grid=(nb, nt),
        in_specs=[pl.BlockSpec((tm, d), rows),
                  pl.BlockSpec((None, 1, 6 * d), lambda b, j: (b, 0, 0)),
                  yspec, yspec, yspec,
                  _resident((None, d, 3 * d), wl3),
                  _resident((None, 3, bw, d), lambda b, j: (l, 0, 0, 0)),
                  _resident((None, d, d), wl3),
                  _resident((None, 1, d), wl3),
                  _resident((None, 1, d), wl3)],
        out_specs=pl.BlockSpec((tm, d), rows),
        out_shape=jax.ShapeDtypeStruct((n, d), F32),
        compiler_params=_params("arbitrary", "arbitrary"),
        name="merge_ln1",
    )(x2d, mod, y_fox, y_sc, y_gla, wp["w_merge"], wp["w_branch"], wp["w_out"], wp["ln1_g"], wp["ln1_b"])


def _ffn_kernel(x_ref, mod_ref, wu_ref, cw_ref, cb_ref, wd_ref, g_ref, b_ref, prev_ref, o_ref, st_ref,
                carry_ref, *, d, dff, chunk, alpha):
    j = pl.program_id(1)
    tm = x_ref.shape[0]
    first = j == 0
    x = x_ref[...]
    hb = (_ln(x) * (1.0 + mod_ref[:, 4 * d:5 * d]) + mod_ref[:, 3 * d:4 * d]).astype(BF16)
    f = jnp.zeros((tm, d), F32)
    for c0 in range(0, dff, chunk):
        cols = slice(c0, c0 + chunk)
        ug = jnp.dot(hb, wu_ref[:, cols], preferred_element_type=F32)
        uv = jnp.dot(hb, wu_ref[:, dff + c0:dff + c0 + chunk], preferred_element_type=F32)
        p0 = jnp.where(first, prev_ref[0:1, cols], carry_ref[SUBLANES - 2:SUBLANES - 1, cols])
        p1 = jnp.where(first, prev_ref[1:2, cols], carry_ref[SUBLANES - 1:SUBLANES, cols])
        ugc = _causal_conv3(ug, p0, p1, cw_ref, cb_ref, cols)
        carry_ref[:, cols] = ug[tm - SUBLANES:tm, :]
        a = (jax.nn.gelu(ugc) * uv).astype(BF16)
        f = f + jnp.dot(a, wd_ref[cols, :], preferred_element_type=F32)
    st_ref[...] = carry_ref[SUBLANES - 2:SUBLANES, :]
    z = alpha * x + mod_ref[:, 5 * d:6 * d] * f
    o_ref[...] = _ln(z) * g_ref[...] + b_ref[...]


def _ffn(l, x2d, mod, wp, prev, nb, seq, alpha):
    n, d = x2d.shape
    dff = wp["w_down"].shape[1]
    tm = _tile(seq, TM_FFN)
    nt = seq // tm
    chunk = _tile(dff, FFN_CHUNK)
    rows = lambda b, j: (b * nt + j, 0)
    wl3 = lambda b, j: (l, 0, 0)
    st = pl.BlockSpec((None, 2, dff), lambda b, j: (b, 0, 0))
    return pl.pallas_call(
        functools.partial(_ffn_kernel, d=d, dff=dff, chunk=chunk, alpha=alpha),
        grid=(nb, nt),
        in_specs=[pl.BlockSpec((tm, d), rows),
                  pl.BlockSpec((None, 1, 6 * d), lambda b, j: (b, 0, 0)),
                  _resident((None, d, 2 * dff), wl3),
                  _resident((None, 3, dff), wl3),
                  _resident((None, 1, dff), wl3),
                  _resident((None, dff, d), wl3),
                  _resident((None, 1, d), wl3),
                  _resident((None, 1, d), wl3),
                  st],
        out_specs=[pl.BlockSpec((tm, d), rows), st],
        out_shape=[jax.ShapeDtypeStruct((n, d), F32), jax.ShapeDtypeStruct((nb, 2, dff), F32)],
        scratch_shapes=[pltpu.VMEM((SUBLANES, dff), F32)],
        compiler_params=_params("arbitrary", "arbitrary"),
        name="conv_ffn_ln2",
    )(x2d, mod, wp["w_up"], wp["ffn_conv_w"], wp["ffn_conv_b"], wp["w_down"], wp["ln2_g"], wp["ln2_b"], prev)


def _pack_weights(w_in, b_fox_f, w_gla_up, b_gla_a, gla_norm_g, sc_conv_w, sc_conv_b, w_branch, w_out,
                  ln1_g, ln1_b, w_up, ffn_conv_w, ffn_conv_b, w_down, ln2_g, ln2_b, heads):
    depth, d, _ = w_in.shape
    fw = sc_conv_w.shape[-1]
    rank = w_gla_up.shape[1]
    o = 0
    fox = w_in[:, :, o:o + 3 * fw]; o += 3 * fw
    wff = w_in[:, :, o:o + heads]; o += heads
    sc = w_in[:, :, o:o + 3 * fw]; o += 3 * fw
    gla = w_in[:, :, o:o + 4 * fw]; o += 4 * fw
    glr = w_in[:, :, o:o + rank]; o += rank
    wm = w_in[:, :, o:]
    row = lambda a: a.reshape(depth, 1, a.shape[-1])
    return {
        "fw": fw,
        "w_main": jnp.concatenate([fox, sc, gla], axis=-1).astype(BF16),
        "w_fft": jnp.swapaxes(wff, 1, 2).astype(BF16),
        "w_glr": jnp.pad(glr, ((0, 0), (0, 0), (0, LANES - rank))).astype(BF16),
        "b_fox": b_fox_f.reshape(depth, heads, 1),
        "w_gla_up": jnp.pad(w_gla_up, ((0, 0), (0, LANES - rank), (0, 0))).astype(BF16),
        "b_gla": row(b_gla_a),
        "gla_norm_g": row(gla_norm_g),
        "sc_w": sc_conv_w, "sc_b": row(sc_conv_b),
        "w_merge": wm.astype(BF16),
        "w_branch": w_branch.astype(BF16),
        "w_out": w_out.astype(BF16),
        "ln1_g": row(ln1_g), "ln1_b": row(ln1_b),
        "w_up": w_up.astype(BF16),
        "ffn_conv_w": ffn_conv_w, "ffn_conv_b": row(ffn_conv_b),
        "w_down": w_down.astype(BF16),
        "ln2_g": row(ln2_g), "ln2_b": row(ln2_b),
    }


def _layer(l, x2d, mod, wp, nb, seq, hd, alpha, fox_cache, sc_prev, gla_s0, ffn_prev, kf_all, vf_all):
    fw = wp["fw"]
    heads = fw // hd
    dk = gla_s0.shape[-2]
    prompt = fox_cache is None
    (q, kf_all, vf_all, kb, v, lft, y_sc, sc_new, gq, gk, gv, gg, la) = _inproj(
        l, x2d, mod, wp, sc_prev, kf_all, vf_all, nb, seq, hd, dk, prompt)
    lf_t = lft.reshape(nb, -1, heads, lft.shape[-1]).transpose(2, 0, 1, 3).reshape(heads, nb, seq)
    logf = lf_t.transpose(1, 2, 0)
    if prompt:
        tq = _tile(seq, TQ_FOX)
        cp, c1, c2, c3 = _cumsum_rows(lf_t.reshape(heads * nb, seq))
        aq_t, ak, cend = _fox_bias_operands(cp, c1, c2, c3, nb, seq, hd, tq)
        y_fox = _fox_prompt(cend, q, aq_t, kb, ak, v, nb, seq, hd, tq)
        y_gla, gla_new = _gla(l, gq, gk, gv, gg, la, gla_s0, wp["gla_norm_g"], nb, seq, False)
    else:
        cache_k, cache_v, lf_cache_nat, lf_cache_t = fox_cache
        y_fox = _fox_cached(l, q, kb, v, cache_k, cache_v, lf_cache_nat, lf_cache_t,
                            logf, lf_t.transpose(1, 0, 2), nb, seq, hd)
        y_gla, gla_new = _gla(l, gq, gk, gv, gg, la, gla_s0, wp["gla_norm_g"], nb, seq, True)
    x1 = _merge(l, x2d, mod, y_fox, y_sc, y_gla, wp, nb, seq, alpha)
    x2, ffn_new = _ffn(l, x1, mod, wp, ffn_prev, nb, seq, alpha)
    return x2, kf_all, vf_all, (logf, sc_new, gla_new, ffn_new)


def kernel(x_prompt, x_sample, c_prompt, c_sample, cache_fox_k, cache_fox_v, cache_fox_logf, state_shortconv, state_gla, state_ffn_conv, w_ada, b_ada, w_in, b_fox_f, w_gla_up, b_gla_a, gla_norm_g, sc_conv_w, sc_conv_b, w_branch, w_out, ln1_g, ln1_b, w_up, ffn_conv_w, ffn_conv_b, w_down, ln2_g, ln2_b):
    bp, sp, d = x_prompt.shape
    bs, ss, _ = x_sample.shape
    depth = w_in.shape[0]
    _, _, past, heads, hd = cache_fox_k.shape
    fw = heads * hd
    dff = ffn_conv_w.shape[-1]
    assert sc_conv_w.shape[-1] == fw and state_gla.shape[2] * state_gla.shape[3] == fw
    assert state_gla.shape[2] * state_gla.shape[4] == fw and fw % LANES == 0 and 2 * hd == LANES
    assert sc_conv_w.shape[1] == 3 and ffn_conv_w.shape[1] == 3
    alpha = (2.0 * depth) ** 0.25

    wp = _pack_weights(w_in, b_fox_f, w_gla_up, b_gla_a, gla_norm_g, sc_conv_w, sc_conv_b, w_branch, w_out,
                       ln1_g, ln1_b, w_up, ffn_conv_w, ffn_conv_b, w_down, ln2_g, ln2_b, heads)
    mods = _ada(jnp.concatenate([c_prompt, c_sample], axis=0), w_ada, b_ada)
    cache_k = cache_fox_k.reshape(depth, bs, past, fw)
    cache_v = cache_fox_v.reshape(depth, bs, past, fw)
    lf_cache_t = jnp.swapaxes(cache_fox_logf, 2, 3)

    xp = x_prompt.reshape(bp * sp, d)
    xs = x_sample.reshape(bs * ss, d)
    zeros_sc = jnp.zeros((bp, 2, fw), F32)
    zeros_gla = jnp.zeros((bp,) + state_gla.shape[2:], F32)
    zeros_ffn = jnp.zeros((bp, 2, dff), F32)
    kp_all = jnp.zeros((depth, bp * sp, fw), F32)
    vp_all = jnp.zeros((depth, bp * sp, fw), F32)
    ks_all = jnp.zeros((depth, bs * ss, fw), F32)
    vs_all = jnp.zeros((depth, bs * ss, fw), F32)
    outs_p = [[] for _ in range(4)]
    outs_s = [[] for _ in range(4)]
    for l in range(depth):
        mod_p = mods[l, :bp].reshape(bp, 1, 6 * d)
        mod_s = mods[l, bp:].reshape(bs, 1, 6 * d)
        xp, kp_all, vp_all, st_p = _layer(l, xp, mod_p, wp, bp, sp, hd, alpha, None,
                                          zeros_sc, zeros_gla, zeros_ffn, kp_all, vp_all)
        xs, ks_all, vs_all, st_s = _layer(l, xs, mod_s, wp, bs, ss, hd, alpha,
                                          (cache_k, cache_v, cache_fox_logf, lf_cache_t),
                                          state_shortconv[l], state_gla, state_ffn_conv[l], ks_all, vs_all)
        for i in range(4):
            outs_p[i].append(st_p[i])
            outs_s[i].append(st_s[i])
    stacked_p = [jnp.stack(o) for o in outs_p]
    stacked_s = [jnp.stack(o) for o in outs_s]
    kv5 = lambda a, nb, seq: a.reshape(depth, nb, seq, heads, hd)
    return (xp.reshape(bp, sp, d), xs.reshape(bs, ss, d),
            kv5(kp_all, bp, sp), kv5(vp_all, bp, sp), *stacked_p,
            kv5(ks_all, bs, ss), kv5(vs_all, bs, ss), *stacked_s)
```

```python
import functools

import jax
import jax.numpy as jnp
from jax import lax
from jax.experimental import pallas as pl
from jax.experimental.pallas import tpu as pltpu

F32 = jnp.float32
BF16 = jnp.bfloat16
LN_EPS = 1e-5
GLA_TAU = 16.0
LANES = 128
SUBLANES = 8
VMEM_LIMIT = 56 * 1024 * 1024
HIGHEST = lax.Precision.HIGHEST
LOG2E = 1.4426950408889634
FOX_UNDERFLOW = 160.0
FOX_NORM_MARGIN = 1.01

TM_PROJ = 512
TM_MERGE = 512
TM_FFN = 512
TQ_FOX = 512
T_GLA = 512
GLA_CHUNK = 128
FFN_CHUNK = 256
ADA_TN = 1024

_NT = (((1,), (1,)), ((), ()))
_TN = (((0,), (0,)), ((), ()))


def _tile(n, pref):
    if n <= pref:
        return n
    t = pref
    while n % t:
        t //= 2
    return t


def _params(*sem):
    return pltpu.CompilerParams(dimension_semantics=sem, vmem_limit_bytes=VMEM_LIMIT)


def _resident(block, index_map):
    return pl.BlockSpec(block, index_map, pipeline_mode=pl.Buffered(1))


def _log_sigmoid(x):
    return jnp.minimum(x, 0.0) - jnp.log1p(jnp.exp(-jnp.abs(x)))


def _ln(x):
    mu = jnp.mean(x, axis=-1, keepdims=True)
    xc = x - mu
    var = jnp.mean(xc * xc, axis=-1, keepdims=True)
    return xc * lax.rsqrt(var + LN_EPS)


def _causal_conv3(u, p0, p1, w_ref, b_ref, cols):
    tm = u.shape[0]
    w0, w1, w2, bias = w_ref[0:1, cols], w_ref[1:2, cols], w_ref[2:3, cols], b_ref[:, cols]

    def taps(u0, u1, u2):
        y = bias + u2 * w0
        y = y + u1 * w1
        return y + u0 * w2

    head = u[0:SUBLANES, :]
    row = lax.broadcasted_iota(jnp.int32, (SUBLANES, 1), 0)
    h1 = jnp.where(row == 0, p1, pltpu.roll(head, 1, 0))
    h2 = jnp.where(row == 0, p0, jnp.where(row == 1, p1, pltpu.roll(head, 2, 0)))
    first = taps(head, h1, h2)
    if tm == SUBLANES:
        return first
    rest = taps(u, pltpu.roll(u, 1, 0), pltpu.roll(u, 2, 0))
    return jnp.concatenate([first, rest[SUBLANES:, :]], axis=0)


def _ada_kernel(c_ref, w_ref, b_ref, o_ref):
    c = c_ref[...]
    s = (c * jax.nn.sigmoid(c)).astype(BF16)
    o_ref[...] = jnp.dot(s, w_ref[...].astype(BF16), preferred_element_type=F32) + b_ref[...]


def _ada(c_all, w_ada, b_ada):
    depth, d, n6 = w_ada.shape
    r = c_all.shape[0]
    tn = _tile(n6, ADA_TN)
    return pl.pallas_call(
        _ada_kernel,
        grid=(depth, n6 // tn),
        in_specs=[pl.BlockSpec((r, d), lambda l, n: (0, 0)),
                  pl.BlockSpec((None, d, tn), lambda l, n: (l, 0, n)),
                  pl.BlockSpec((None, 1, tn), lambda l, n: (l, 0, n))],
        out_specs=pl.BlockSpec((None, r, tn), lambda l, n: (l, 0, n)),
        out_shape=jax.ShapeDtypeStruct((depth, r, n6), F32),
        compiler_params=_params("arbitrary", "arbitrary"),
        name="ada_mod",
    )(c_all, w_ada, b_ada.reshape(depth, 1, n6))


def _inproj_kernel(x_ref, mod_ref, w_ref, wff_ref, wglr_ref, bff_ref, wup_ref, ba_ref, scw_ref, scb_ref,
                   prev_ref, kf_all_ref, vf_all_ref,
                   q_ref, kf_ref, vf_ref, kb_ref, vb_ref, lf_ref, ysc_ref, scst_ref,
                   gq_ref, gk_ref, gv_ref, gg_ref, la_ref,
                   carry_ref, *, d, fw, q_scale, gq_scale, transposed):
    del kf_all_ref, vf_all_ref
    j = pl.program_id(1)
    tm = x_ref.shape[0]
    h = _ln(x_ref[...]) * (1.0 + mod_ref[:, d:2 * d]) + mod_ref[:, 0:d]
    hb = h.astype(BF16)

    def proj(g):
        return jnp.dot(hb, w_ref[:, g * fw:(g + 1) * fw], preferred_element_type=F32)

    q = proj(0) * q_scale
    k = proj(1)
    v = proj(2)
    kb_ref[...] = k.astype(BF16)
    if transposed:
        vt = v.T
        q_ref[...] = q.T.astype(BF16)
        kf_ref[...] = k.T
        vf_ref[...] = vt
        vb_ref[...] = vt.astype(BF16)
    else:
        q_ref[...] = q.astype(BF16)
        kf_ref[...] = k
        vf_ref[...] = v
        vb_ref[...] = v.astype(BF16)
    fft =lax.dot_general(wff_ref[...], hb, _NT, preferred_element_type=F32)
    lf_ref[...] = _log_sigmoid(fft + bff_ref[...])

    first = j == 0
    p0 = jnp.where(first, prev_ref[0:1, :], carry_ref[SUBLANES - 2:SUBLANES - 1, :])
    p1 = jnp.where(first, prev_ref[1:2, :], carry_ref[SUBLANES - 1:SUBLANES, :])
    sb = proj(3)
    u = proj(4) * proj(5)
    y = _causal_conv3(u, p0, p1, scw_ref, scb_ref, slice(None))
    ysc_ref[...] = (sb * y).astype(BF16)
    carry_ref[...] = u[tm - SUBLANES:tm, :]
    scst_ref[...] = carry_ref[SUBLANES - 2:SUBLANES, :]

    gq_ref[...] = (proj(6) * gq_scale).astype(BF16)
    gk_ref[...] = proj(7).astype(BF16)
    gv_ref[...] = proj(8).astype(BF16)
    gg = proj(9)
    gg_ref[...] = (gg * jax.nn.sigmoid(gg)).astype(BF16)
    glr = jnp.dot(hb, wglr_ref[...], preferred_element_type=F32)
    lap = jnp.dot(glr.astype(BF16), wup_ref[...], preferred_element_type=F32) + ba_ref[...]
    la_ref[...] = _log_sigmoid(lap) / GLA_TAU


def _inproj(l, x2d, mod, wp, sc_prev, kf_all, vf_all, nb, seq, hd, dk, transposed):
    n, d = x2d.shape
    fw = wp["fw"]
    heads = fw // hd
    tm = _tile(seq, TM_PROJ)
    nt = seq // tm
    rows = lambda b, j: (b * nt + j, 0)
    wl = lambda b, j: (l, 0, 0)
    act = lambda dt: jax.ShapeDtypeStruct((n, fw), dt)
    act_spec = pl.BlockSpec((tm, fw), rows)
    if transposed:
        t_shape, t_spec = jax.ShapeDtypeStruct((fw, n), BF16), pl.BlockSpec((fw, tm), lambda b, j: (0, b * nt + j))
        q_scale = LOG2E * hd ** -0.5
        stacked = pl.BlockSpec((None, None, fw, tm), lambda b, j: (l, b, 0, j))
    else:
        t_shape, t_spec, q_scale = act(BF16), act_spec, hd ** -0.5
        stacked = pl.BlockSpec((None, tm, fw), lambda b, j: (l, b * nt + j, 0))
    hbm = pl.BlockSpec(memory_space=pl.ANY)
    kern = functools.partial(_inproj_kernel, d=d, fw=fw, q_scale=q_scale, gq_scale=dk ** -0.5,
                             transposed=transposed)
    return pl.pallas_call(
        kern,
        grid=(nb, nt),
        in_specs=[pl.BlockSpec((tm, d), rows),
                  pl.BlockSpec((None, 1, 6 * d), lambda b, j: (b, 0, 0)),
                  _resident((None, d, 10 * fw), wl),
                  _resident((None, heads, d), wl),
                  _resident((None, d, LANES), wl),
                  _resident((None, heads, 1), wl),
                  _resident((None, LANES, fw), wl),
                  _resident((None, 1, fw), wl),
                  _resident((None, 3, fw), wl),
                  _resident((None, 1, fw), wl),
                  pl.BlockSpec((None, 2, fw), lambda b, j: (b, 0, 0)),
                  hbm, hbm],
        out_specs=[t_spec, stacked, stacked, act_spec, t_spec,
                   pl.BlockSpec((None, heads, tm), lambda b, j: (b * nt + j, 0, 0)),
                   act_spec,
                   pl.BlockSpec((None, 2, fw), lambda b, j: (b, 0, 0)),
                   act_spec, act_spec, act_spec, act_spec, act_spec],
        out_shape=[t_shape, jax.ShapeDtypeStruct(kf_all.shape, F32), jax.ShapeDtypeStruct(vf_all.shape, F32),
                   act(BF16), t_shape,
                   jax.ShapeDtypeStruct((nb * nt, heads, tm), F32),
                   act(BF16),
                   jax.ShapeDtypeStruct((nb, 2, fw), F32),
                   act(BF16), act(BF16), act(BF16), act(BF16), act(F32)],
        input_output_aliases={11: 1, 12: 2},
        scratch_shapes=[pltpu.VMEM((SUBLANES, fw), F32)],
        compiler_params=_params("arbitrary", "arbitrary"),
        name="in_proj",
    )(x2d, mod, wp["w_main"], wp["w_fft"], wp["w_glr"], wp["b_fox"], wp["w_gla_up"], wp["b_gla"],
      wp["sc_w"], wp["sc_b"], sc_prev, kf_all, vf_all)


def _lane_cumsum(x, lane):
    for k in range(7):
        s = 1 << k
        x = x + jnp.where(lane >= s, pltpu.roll(x, s, 1), 0.0)
    return x


def _cumsum_kernel(x_ref, cp_ref, c1_ref, c2_ref, c3_ref):
    r, length = x_ref.shape
    lane = lax.broadcasted_iota(jnp.int32, (r, LANES), 1)

    def body(i, carry):
        off = pl.multiple_of(i * LANES, LANES)
        x = _lane_cumsum(x_ref[:, pl.ds(off, LANES)], lane) + carry
        c = x * LOG2E
        c1 = c.astype(BF16).astype(F32)
        r1 = c - c1
        c2 = r1.astype(BF16).astype(F32)
        c3 = (r1 - c2).astype(BF16).astype(F32)
        cp_ref[:, pl.ds(off, LANES)] = c
        c1_ref[:, pl.ds(off, LANES)] = c1
        c2_ref[:, pl.ds(off, LANES)] = c2
        c3_ref[:, pl.ds(off, LANES)] = c3
        return x[:, LANES - 1:LANES]

    lax.fori_loop(0, length // LANES, body, jnp.zeros((r, 1), F32), unroll=min(8, length // LANES))


def _cumsum_rows(x):
    r, length = x.shape
    spec = pl.BlockSpec((r, length), lambda i: (0, 0))
    return pl.pallas_call(
        _cumsum_kernel,
        grid=(1,),
        in_specs=[spec],
        out_specs=[spec, spec, spec, spec],
        out_shape=[jax.ShapeDtypeStruct((r, length), F32)] * 4,
        compiler_params=_params("arbitrary"),
        name="fox_cumsum",
    )(x)


def _fox_kernel(cend_ref, qt_ref, cq1_ref, cq2_ref, cq3_ref, k_ref, ck1_ref, ck2_ref, ck3_ref, vt_ref, o_ref,
                m_ref, acc_ref, qs_ref, ak_ref, kmax_ref, *, hd):
    b = pl.program_id(0)
    pr = pl.program_id(1)
    i = pl.program_id(2)
    tq = qt_ref.shape[1]
    seq = k_ref.shape[0]
    nk = seq // tq
    lane = lax.broadcasted_iota(jnp.int32, (1, LANES), 1)
    head_a = lane < hd
    cqs = (cq1_ref, cq2_ref, cq3_ref)
    cks = (ck1_ref, ck2_ref, ck3_ref)

    @pl.when(i == 0)
    def _():
        r = lax.broadcasted_iota(jnp.int32, (LANES, LANES), 0)
        ones_rows = jnp.logical_or(r < 3, jnp.logical_and(r >= hd, r < hd + 3))
        base_m = jnp.where(ones_rows, 1.0, 0.0)

        def chunk(t, carry):
            off = pl.multiple_of(t * LANES, LANES)
            m = base_m
            for n, ck in enumerate(cks):
                m = jnp.where(r == 3 + n, -ck[1, :, pl.ds(off, LANES)], m)
                m = jnp.where(r == hd + 3 + n, -ck[0, :, pl.ds(off, LANES)], m)
            ak_ref[pl.ds(off, LANES), :] = m.T.astype(BF16)
            return carry

        lax.fori_loop(0, seq // LANES, chunk, 0)
        kk = k_ref[...].astype(F32)
        kk = kk * kk
        for h in range(2):
            sel = head_a if h == 0 else jnp.logical_not(head_a)
            n2 = jnp.sum(jnp.where(sel, kk, 0.0), axis=-1, keepdims=True)
            kmax_ref[h] = jnp.sqrt(jnp.max(n2))

    group = 2 * SUBLANES
    rg = lax.broadcasted_iota(jnp.int32, (group, tq), 0)

    def bias_rows(h):
        rows = jnp.where(rg < 6, 1.0, 0.0)
        for n, cq in enumerate(cqs):
            rows = jnp.where(rg == n, cq[h], rows)
        return rows.astype(BF16)

    qs_ref[0, 0:hd, :] = qt_ref[0:hd, :]
    qs_ref[0, hd:hd + group, :] = bias_rows(0)
    qs_ref[0, hd + group:, :] = jnp.zeros((hd - group, tq), BF16)
    qs_ref[1, 0:group, :] = bias_rows(1)
    qs_ref[1, group:hd, :] = jnp.zeros((hd - group, tq), BF16)
    qs_ref[1, hd:, :] = qt_ref[hd:, :]
    m_ref[...] = jnp.full(m_ref.shape, -jnp.inf, F32)
    acc_ref[...] = jnp.zeros(acc_ref.shape, F32)

    def step(jk, tk, diagonal):
        off = pl.multiple_of(jk * tq, tq)
        ones = jnp.ones((hd, tk), BF16)
        k = k_ref[pl.ds(off, tk), :]
        ak = ak_ref[pl.ds(off, tk), :]
        ks = (jnp.where(head_a, k, ak), jnp.where(head_a, ak, k))
        vts = (jnp.concatenate([vt_ref[0:hd, pl.ds(off, tk)], ones], axis=0),
               jnp.concatenate([ones, vt_ref[hd:, pl.ds(off, tk)]], axis=0))
        sts = [jnp.dot(ks[h], qs_ref[h], preferred_element_type=F32) for h in range(2)]
        if diagonal:
            krow = lax.broadcasted_iota(jnp.int32, (tk, tq), 0)
            qcol = lax.broadcasted_iota(jnp.int32, (tk, tq), 1)
            sts = [jnp.where(krow <= qcol, st, -jnp.inf) for st in sts]
        m_prevs = [m_ref[h] for h in range(2)]
        m_news = [jnp.maximum(m_prevs[h], jnp.max(sts[h], axis=0, keepdims=True)) for h in range(2)]
        for h in range(2):
            p = jnp.exp2(sts[h] - m_news[h]).astype(BF16)
            alpha = jnp.exp2(m_prevs[h] - m_news[h])
            acc_ref[h] = alpha * acc_ref[h] + jnp.dot(vts[h], p, preferred_element_type=F32)
            m_ref[h] = m_news[h]

    step(i, tq, True)

    base = ((b * pl.num_programs(1) + pr) * 2) * nk
    slack = []
    for h in range(2):
        qf = qt_ref[h * hd:(h + 1) * hd, :].astype(F32)
        qn = jnp.sqrt(jnp.max(jnp.sum(qf * qf, axis=0, keepdims=True)))
        slack.append(qn * kmax_ref[h] * FOX_NORM_MARGIN - jnp.min(m_ref[h]))
    before = jnp.maximum(i - 1, 0)

    def needed(j):
        r = False
        for h in range(2):
            gap = cend_ref[base + h * nk + before] - cend_ref[base + h * nk + j]
            r = jnp.logical_or(r, slack[h] + gap > -FOX_UNDERFLOW)
        return r

    j_lo = lax.while_loop(lambda j: jnp.logical_and(j > 0, needed(jnp.maximum(j - 1, 0))), lambda j: j - 1, i)
    n_off = i - j_lo

    def body(t, carry):
        step(j_lo + 2 * t, 2 * tq, False)
        return carry

    lax.fori_loop(0, n_off // 2, body, 0)

    @pl.when(n_off % 2 == 1)
    def _():
        step(i - 1, tq, False)

    acc_a = acc_ref[0]
    acc_b = acc_ref[1]
    out_t = jnp.concatenate([acc_a[0:hd] / acc_a[hd:hd + 1], acc_b[hd:] / acc_b[0:1]], axis=0)
    o_ref[...] = out_t.T.astype(BF16)


def _fox_prompt(cp, c1, c2, c3, qt, kb, vt, nb, seq, hd):
    fw, n = qt.shape
    pairs = fw // LANES
    tq = _tile(seq, TQ_FOX)
    nq = seq // tq
    by_pair = lambda a: a.reshape(pairs, 2, nb, 1, seq)
    cend = by_pair(cp)[:, :, :, 0, tq - 1::tq].transpose(2, 0, 1, 3).reshape(-1)
    terms = [by_pair(c) for c in (c1, c2, c3)]
    qmap = lambda b, p, i, c: (p, b * nq + i)
    kmap = lambda b, p, i, c: (b, p)
    cq_spec = pl.BlockSpec((None, 2, None, 1, tq), lambda b, p, i, c: (p, 0, b, 0, i))
    ck_spec = pl.BlockSpec((None, 2, None, 1, seq), lambda b, p, i, c: (p, 0, b, 0, 0))
    grid_spec = pltpu.PrefetchScalarGridSpec(
        num_scalar_prefetch=1,
        grid=(nb, pairs, nq),
        in_specs=[pl.BlockSpec((LANES, tq), qmap), cq_spec, cq_spec, cq_spec,
                  pl.BlockSpec((seq, LANES), kmap), ck_spec, ck_spec, ck_spec,
                  pl.BlockSpec((LANES, seq), lambda b, p, i, c: (p, b))],
        out_specs=pl.BlockSpec((tq, LANES), lambda b, p, i, c: (b * nq + i, p)),
        scratch_shapes=[pltpu.VMEM((2, 1, tq), F32), pltpu.VMEM((2, LANES, tq), F32),
                        pltpu.VMEM((2, LANES, tq), BF16), pltpu.VMEM((seq, LANES), BF16),
                        pltpu.SMEM((2,), F32)],
    )
    return pl.pallas_call(
        functools.partial(_fox_kernel, hd=hd),
        grid_spec=grid_spec,
        out_shape=jax.ShapeDtypeStruct((n, fw), BF16),
        compiler_params=_params("arbitrary", "arbitrary", "arbitrary"),
        name="fox_attn",
    )(cend, qt, *terms, kb, *terms, vt)


def _fox_cached_kernel(q_ref, kn_ref, vn_ref, kc_ref, vc_ref, lcn_ref, lct_ref, lnn_ref, lnt_ref, o_ref,
                       ck_ref, *, hd):
    ln, fw = q_ref.shape
    past = kc_ref.shape[0]
    heads = fw // hd
    lane8 = lax.broadcasted_iota(jnp.int32, (heads, LANES), 1)

    def body(i, carry):
        off = pl.multiple_of(i * LANES, LANES)
        x = _lane_cumsum(lct_ref[:, pl.ds(off, LANES)], lane8) + carry
        ck_ref[:, pl.ds(off, LANES)] = x
        return x[:, LANES - 1:LANES]

    c_last = lax.fori_loop(0, past // LANES, body, jnp.zeros((heads, 1), F32),
                           unroll=min(8, past // LANES))
    r_i = lax.broadcasted_iota(jnp.int32, (ln, ln), 0)
    c_i = lax.broadcasted_iota(jnp.int32, (ln, ln), 1)
    causal = c_i <= r_i
    tril = causal.astype(F32)
    triu = (r_i <= c_i).astype(F32)
    ck_new = c_last + jnp.dot(lnt_ref[...], triu, precision=HIGHEST, preferred_element_type=F32)
    cache_total = jnp.dot(jnp.ones((ln, past), F32), lcn_ref[...], precision=HIGHEST,
                          preferred_element_type=F32)
    cq = cache_total + jnp.dot(tril, lnn_ref[...], precision=HIGHEST, preferred_element_type=F32)

    lane = lax.broadcasted_iota(jnp.int32, (1, LANES), 1)
    head_a = lane < hd
    for p in range(fw // LANES):
        cols = slice(p * LANES, (p + 1) * LANES)
        q = q_ref[:, cols]
        zero = jnp.zeros_like(q)
        kc = kc_ref[:, cols].astype(BF16)
        vc = vc_ref[:, cols].astype(BF16)
        kn = kn_ref[:, cols]
        vn = vn_ref[:, cols]
        outs = []
        for hh in range(2):
            h = 2 * p + hh
            qh = jnp.where(head_a, q, zero) if hh == 0 else jnp.where(head_a, zero, q)
            cqh = cq[:, h:h + 1]
            s_c = lax.dot_general(qh, kc, _NT, preferred_element_type=F32) + cqh - ck_ref[h:h + 1, :]
            s_n = lax.dot_general(qh, kn, _NT, preferred_element_type=F32) + cqh - ck_new[h:h + 1, :]
            s_n = jnp.where(causal, s_n, -jnp.inf)
            m = jnp.maximum(jnp.max(s_c, axis=-1, keepdims=True), jnp.max(s_n, axis=-1, keepdims=True))
            p_c = jnp.exp(s_c - m)
            p_n = jnp.exp(s_n - m)
            den = jnp.sum(p_c, axis=-1, keepdims=True) + jnp.sum(p_n, axis=-1, keepdims=True)
            acc = (jnp.dot(p_c.astype(BF16), vc, preferred_element_type=F32)
                   + jnp.dot(p_n.astype(BF16), vn, preferred_element_type=F32))
            outs.append(acc / den)
        o_ref[:, cols] = jnp.where(head_a, outs[0], outs[1]).astype(BF16)


def _fox_cached(l, qb, kb, vb, cache_k, cache_v, lf_cache_nat, lf_cache_t, lf_new_nat, lf_new_t, nb, ln, hd):
    n, fw = qb.shape
    past = cache_k.shape[2]
    heads = fw // hd
    rows = lambda b: (b, 0)
    cache = lambda b: (l, b, 0, 0)
    act = pl.BlockSpec((ln, fw), rows)
    return pl.pallas_call(
        functools.partial(_fox_cached_kernel, hd=hd),
        grid=(nb,),
        in_specs=[act, act, act,
                  pl.BlockSpec((None, None, past, fw), cache),
                  pl.BlockSpec((None, None, past, fw), cache),
                  pl.BlockSpec((None, None, past, heads), cache),
                  pl.BlockSpec((None, None, heads, past), cache),
                  pl.BlockSpec((None, ln, heads), lambda b: (b, 0, 0)),
                  pl.BlockSpec((None, heads, ln), lambda b: (b, 0, 0))],
        out_specs=act,
        out_shape=jax.ShapeDtypeStruct((n, fw), BF16),
        scratch_shapes=[pltpu.VMEM((heads, past), F32)],
        compiler_params=_params("arbitrary"),
        name="fox_attn_cached",
    )(qb, kb, vb, cache_k, cache_v, lf_cache_nat, lf_cache_t, lf_new_nat, lf_new_t)


def _gla_kernel(q_ref, k_ref, v_ref, g_ref, la_ref, s0_ref, gn_ref, y_ref, sout_ref, st_ref,
                *, chunk, nh, dk, dv):
    j = pl.program_id(1)
    t_rows = q_ref.shape[0]

    @pl.when(j == 0)
    def _():
        for h in range(nh):
            st_ref[h] = s0_ref[h].T

    row = lax.broadcasted_iota(jnp.int32, (chunk, chunk), 0)
    col = lax.broadcasted_iota(jnp.int32, (chunk, chunk), 1)
    causal = col <= row
    tril = causal.astype(F32)
    mid = chunk // 2
    for c in range(t_rows // chunk):
        rows = slice(c * chunk, (c + 1) * chunk)
        for h in range(nh):
            kc = slice(h * dk, (h + 1) * dk)
            vc = slice(h * dv, (h + 1) * dv)
            b = jnp.dot(tril, la_ref[rows, kc], precision=HIGHEST, preferred_element_type=F32)
            b_last = b[chunk - 1:chunk, :]
            b_mid = b[mid:mid + 1, :]
            q = q_ref[rows, kc].astype(F32)
            k = k_ref[rows, kc].astype(F32)
            v = v_ref[rows, vc]
            qi = (q * jnp.exp(b - b_mid)).astype(BF16)
            ki = (k * jnp.exp(b_mid - b)).astype(BF16)
            a = lax.dot_general(qi, ki, _NT, preferred_element_type=F32)
            a = jnp.where(causal, a, 0.0)
            o = jnp.dot(a.astype(BF16), v, preferred_element_type=F32)
            st = st_ref[h]
            qe = (q * jnp.exp(b)).astype(BF16)
            o = o + lax.dot_general(qe, st.astype(BF16), _NT, preferred_element_type=F32)
            kd = (k * jnp.exp(b_last - b)).astype(BF16)
            st_ref[h] = st * jnp.exp(b_last) + lax.dot_general(v, kd, _TN, preferred_element_type=F32)
            ms = jnp.mean(o * o, axis=-1, keepdims=True)
            y = o * lax.rsqrt(ms + LN_EPS) * gn_ref[...]
            y_ref[rows, vc] = (y * g_ref[rows, vc].astype(F32)).astype(BF16)

    @pl.when(j == pl.num_programs(1) - 1)
    def _():
        for h in range(nh):
            sout_ref[h] = st_ref[h].T


def _gla(l, gq, gk, gv, gg, la, s0, gn, nb, seq, s0_layered):
    n, kw = gq.shape
    nh, dk, dv = s0.shape[-3:]
    t_rows = _tile(seq, T_GLA)
    chunk = _tile(t_rows, GLA_CHUNK)
    nt = seq // t_rows
    rows = lambda b, j: (b * nt + j, 0)
    act = pl.BlockSpec((t_rows, kw), rows)
    if s0_layered:
        s0_spec = pl.BlockSpec((None, None, nh, dk, dv), lambda b, j: (l, b, 0, 0, 0))
    else:
        s0_spec = pl.BlockSpec((None, nh, dk, dv), lambda b, j: (b, 0, 0, 0))
    return pl.pallas_call(
        functools.partial(_gla_kernel, chunk=chunk, nh=nh, dk=dk, dv=dv),
        grid=(nb, nt),
        in_specs=[act, act, act, act, act, s0_spec,
                  _resident((None, 1, dv), lambda b, j: (l, 0, 0))],
        out_specs=[act, pl.BlockSpec((None, nh, dk, dv), lambda b, j: (b, 0, 0, 0))],
        out_shape=[jax.ShapeDtypeStruct((n, kw), BF16), jax.ShapeDtypeStruct((nb, nh, dk, dv), F32)],
        scratch_shapes=[pltpu.VMEM((nh, dv, dk), F32)],
        compiler_params=_params("arbitrary", "arbitrary"),
        name="gla",
    )(gq, gk, gv, gg, la, s0, gn)


def _merge_kernel(x_ref, mod_ref, yf_ref, ys_ref, yg_ref, wm_ref, wb_ref, wo_ref, g_ref, b_ref, o_ref,
                  *, d, alpha):
    x = x_ref[...]
    hb = (_ln(x) * (1.0 + mod_ref[:, d:2 * d]) + mod_ref[:, 0:d]).astype(BF16)
    merged = None
    for i, y_ref in enumerate((yf_ref, ys_ref, yg_ref)):
        gate = jax.nn.sigmoid(jnp.dot(hb, wm_ref[:, i * d:(i + 1) * d], preferred_element_type=F32))
        term = gate * jnp.dot(y_ref[...], wb_ref[i], preferred_element_type=F32)
        merged = term if merged is None else merged + term
    mix = jnp.dot(merged.astype(BF16), wo_ref[...], preferred_element_type=F32)
    z = alpha * x + mod_ref[:, 2 * d:3 * d] * mix
    o_ref[...] = _ln(z) * g_ref[...] + b_ref[...]


def _merge(l, x2d, mod, y_fox, y_sc, y_gla, wp, nb, seq, alpha):
    n, d = x2d.shape
    bw = y_fox.shape[1]
    tm = _tile(seq, TM_MERGE)
    nt = seq // tm
    rows = lambda b, j: (b * nt + j, 0)
    wl3 = lambda b, j: (l, 0, 0)
    yspec = pl.BlockSpec((tm, bw), rows)
    return pl.pallas_call(
        functools.partial(_merge_kernel, d=d, alpha=alpha),
        grid=(nb, nt),
        in_specs=[pl.BlockSpec((tm, d), rows),
                  pl.BlockSpec((None, 1, 6 * d), lambda b, j: (b, 0, 0)),
                  yspec, yspec, yspec,
                  _resident((None, d, 3 * d), wl3),
                  _resident((None, 3, bw, d), lambda b, j: (l, 0, 0, 0)),
                  _resident((None, d, d), wl3),
                  _resident((None, 1, d), wl3),
                  _resident((None, 1, d), wl3)],
        out_specs=pl.BlockSpec((tm, d), rows),
        out_shape=jax.ShapeDtypeStruct((n, d), F32),
        compiler_params=_params("arbitrary", "arbitrary"),
        name="merge_ln1",
    )(x2d, mod, y_fox, y_sc, y_gla, wp["w_merge"], wp["w_branch"], wp["w_out"], wp["ln1_g"], wp["ln1_b"])


def _ffn_kernel(x_ref, mod_ref, wu_ref, cw_ref, cb_ref, wd_ref, g_ref, b_ref, prev_ref, o_ref, st_ref,
                carry_ref, act_ref, *, d, dff, chunk, alpha):
    j = pl.program_id(1)
    tm = x_ref.shape[0]
    first = j == 0
    x = x_ref[...]
    hb = (_ln(x) * (1.0 + mod_ref[:, 4 * d:5 * d]) + mod_ref[:, 3 * d:4 * d]).astype(BF16)
    for c0 in range(0, dff, chunk):
        cols = slice(c0, c0 + chunk)
        ug = jnp.dot(hb, wu_ref[:, cols], preferred_element_type=F32)
        uv = jnp.dot(hb, wu_ref[:, dff + c0:dff + c0 + chunk], preferred_element_type=F32)
        p0 = jnp.where(first, prev_ref[0:1, cols], carry_ref[SUBLANES - 2:SUBLANES - 1, cols])
        p1 = jnp.where(first, prev_ref[1:2, cols], carry_ref[SUBLANES - 1:SUBLANES, cols])
        ugc = _causal_conv3(ug, p0, p1, cw_ref, cb_ref, cols)
        carry_ref[:, cols] = ug[tm - SUBLANES:tm, :]
        act_ref[:, cols] = (jax.nn.gelu(ugc) * uv).astype(BF16)
    f = jnp.dot(act_ref[...], wd_ref[...], preferred_element_type=F32)
    st_ref[...] = carry_ref[SUBLANES - 2:SUBLANES, :]
    z = alpha * x + mod_ref[:, 5 * d:6 * d] * f
    o_ref[...] = _ln(z) * g_ref[...] + b_ref[...]


def _ffn(l, x2d, mod, wp, prev, nb, seq, alpha):
    n, d = x2d.shape
    dff = wp["w_down"].shape[1]
    tm = _tile(seq, TM_FFN)
    nt = seq // tm
    chunk = _tile(dff, FFN_CHUNK)
    rows = lambda b, j: (b * nt + j, 0)
    wl3 = lambda b, j: (l, 0, 0)
    st = pl.BlockSpec((None, 2, dff), lambda b, j: (b, 0, 0))
    return pl.pallas_call(
        functools.partial(_ffn_kernel, d=d, dff=dff, chunk=chunk, alpha=alpha),
        grid=(nb, nt),
        in_specs=[pl.BlockSpec((tm, d), rows),
                  pl.BlockSpec((None, 1, 6 * d), lambda b, j: (b, 0, 0)),
                  _resident((None, d, 2 * dff), wl3),
                  _resident((None, 3, dff), wl3),
                  _resident((None, 1, dff), wl3),
                  _resident((None, dff, d), wl3),
                  _resident((None, 1, d), wl3),
                  _resident((None, 1, d), wl3),
                  st],
        out_specs=[pl.BlockSpec((tm, d), rows), st],
        out_shape=[jax.ShapeDtypeStruct((n, d), F32), jax.ShapeDtypeStruct((nb, 2, dff), F32)],
        scratch_shapes=[pltpu.VMEM((SUBLANES, dff), F32), pltpu.VMEM((tm, dff), BF16)],
        compiler_params=_params("arbitrary", "arbitrary"),
        name="conv_ffn_ln2",
    )(x2d, mod, wp["w_up"], wp["ffn_conv_w"], wp["ffn_conv_b"], wp["w_down"], wp["ln2_g"], wp["ln2_b"], prev)


def _pack_weights(w_in, b_fox_f, w_gla_up, b_gla_a, gla_norm_g, sc_conv_w, sc_conv_b, w_branch, w_out,
                  ln1_g, ln1_b, w_up, ffn_conv_w, ffn_conv_b, w_down, ln2_g, ln2_b, heads):
    depth, d, _ = w_in.shape
    fw = sc_conv_w.shape[-1]
    rank = w_gla_up.shape[1]
    o = 0
    fox = w_in[:, :, o:o + 3 * fw]; o += 3 * fw
    wff = w_in[:, :, o:o + heads]; o += heads
    sc = w_in[:, :, o:o + 3 * fw]; o += 3 * fw
    gla = w_in[:, :, o:o + 4 * fw]; o += 4 * fw
    glr = w_in[:, :, o:o + rank]; o += rank
    wm = w_in[:, :, o:]
    row = lambda a: a.reshape(depth, 1, a.shape[-1])
    return {
        "fw": fw,
        "w_main": jnp.concatenate([fox, sc, gla], axis=-1).astype(BF16),
        "w_fft": jnp.swapaxes(wff, 1, 2).astype(BF16),
        "w_glr": jnp.pad(glr, ((0, 0), (0, 0), (0, LANES - rank))).astype(BF16),
        "b_fox": b_fox_f.reshape(depth, heads, 1),
        "w_gla_up": jnp.pad(w_gla_up, ((0, 0), (0, LANES - rank), (0, 0))).astype(BF16),
        "b_gla": row(b_gla_a),
        "gla_norm_g": row(gla_norm_g),
        "sc_w": sc_conv_w, "sc_b": row(sc_conv_b),
        "w_merge": wm.astype(BF16),
        "w_branch": w_branch.astype(BF16),
        "w_out": w_out.astype(BF16),
        "ln1_g": row(ln1_g), "ln1_b": row(ln1_b),
        "w_up": w_up.astype(BF16),
        "ffn_conv_w": ffn_conv_w, "ffn_conv_b": row(ffn_conv_b),
        "w_down": w_down.astype(BF16),
        "ln2_g": row(ln2_g), "ln2_b": row(ln2_b),
    }


def _layer(l, x2d, mod, wp, nb, seq, hd, alpha, fox_cache, sc_prev, gla_s0, ffn_prev, kf_all, vf_all):
    fw = wp["fw"]
    heads = fw // hd
    dk = gla_s0.shape[-2]
    prompt = fox_cache is None
    (q, kf_all, vf_all, kb, v, lft, y_sc, sc_new, gq, gk, gv, gg, la) = _inproj(
        l, x2d, mod, wp, sc_prev, kf_all, vf_all, nb, seq, hd, dk, prompt)
    lf_t = lft.reshape(nb, -1, heads, lft.shape[-1]).transpose(2, 0, 1, 3).reshape(heads, nb, seq)
    logf = lf_t.transpose(1, 2, 0)
    if prompt:
        cp, c1, c2, c3 = _cumsum_rows(lf_t.reshape(heads * nb, seq))
        y_fox = _fox_prompt(cp, c1, c2, c3, q, kb, v, nb, seq, hd)
        y_gla, gla_new = _gla(l, gq, gk, gv, gg, la, gla_s0, wp["gla_norm_g"], nb, seq, False)
    else:
        cache_k, cache_v, lf_cache_nat, lf_cache_t = fox_cache
        y_fox = _fox_cached(l, q, kb, v, cache_k, cache_v, lf_cache_nat, lf_cache_t,
                            logf, lf_t.transpose(1, 0, 2), nb, seq, hd)
        y_gla, gla_new = _gla(l, gq, gk, gv, gg, la, gla_s0, wp["gla_norm_g"], nb, seq, True)
    x1 = _merge(l, x2d, mod, y_fox, y_sc, y_gla, wp, nb, seq, alpha)
    x2, ffn_new = _ffn(l, x1, mod, wp, ffn_prev, nb, seq, alpha)
    return x2, kf_all, vf_all, (logf, sc_new, gla_new, ffn_new)


def kernel(x_prompt, x_sample, c_prompt, c_sample, cache_fox_k, cache_fox_v, cache_fox_logf, state_shortconv, state_gla, state_ffn_conv, w_ada, b_ada, w_in, b_fox_f, w_gla_up, b_gla_a, gla_norm_g, sc_conv_w, sc_conv_b, w_branch, w_out, ln1_g, ln1_b, w_up, ffn_conv_w, ffn_conv_b, w_down, ln2_g, ln2_b):
    bp, sp, d = x_prompt.shape
    bs, ss, _ = x_sample.shape
    depth = w_in.shape[0]
    _, _, past, heads, hd = cache_fox_k.shape
    fw = heads * hd
    dff = ffn_conv_w.shape[-1]
    assert sc_conv_w.shape[-1] == fw and state_gla.shape[2] * state_gla.shape[3] == fw
    assert state_gla.shape[2] * state_gla.shape[4] == fw and fw % LANES == 0 and 2 * hd == LANES
    assert sc_conv_w.shape[1] == 3 and ffn_conv_w.shape[1] == 3
    alpha = (2.0 * depth) ** 0.25

    wp = _pack_weights(w_in, b_fox_f, w_gla_up, b_gla_a, gla_norm_g, sc_conv_w, sc_conv_b, w_branch, w_out,
                       ln1_g, ln1_b, w_up, ffn_conv_w, ffn_conv_b, w_down, ln2_g, ln2_b, heads)
    mods = _ada(jnp.concatenate([c_prompt, c_sample], axis=0), w_ada, b_ada)
    cache_k = cache_fox_k.reshape(depth, bs, past, fw)
    cache_v = cache_fox_v.reshape(depth, bs, past, fw)
    lf_cache_t = jnp.swapaxes(cache_fox_logf, 2, 3)

    xp = x_prompt.reshape(bp * sp, d)
    xs = x_sample.reshape(bs * ss, d)
    zeros_sc = jnp.zeros((bp, 2, fw), F32)
    zeros_gla = jnp.zeros((bp,) + state_gla.shape[2:], F32)
    zeros_ffn = jnp.zeros((bp, 2, dff), F32)
    kp_all = jnp.zeros((depth, bp, fw, sp), F32)
    vp_all = jnp.zeros((depth, bp, fw, sp), F32)
    ks_all = jnp.zeros((depth, bs * ss, fw), F32)
    vs_all = jnp.zeros((depth, bs * ss, fw), F32)
    outs_p = [[] for _ in range(4)]
    outs_s = [[] for _ in range(4)]
    for l in range(depth):
        mod_p = mods[l, :bp].reshape(bp, 1, 6 * d)
        mod_s = mods[l, bp:].reshape(bs, 1, 6 * d)
        xp, kp_all, vp_all, st_p = _layer(l, xp, mod_p, wp, bp, sp, hd, alpha, None,
                                          zeros_sc, zeros_gla, zeros_ffn, kp_all, vp_all)
        xs, ks_all, vs_all, st_s = _layer(l, xs, mod_s, wp, bs, ss, hd, alpha,
                                          (cache_k, cache_v, cache_fox_logf, lf_cache_t),
                                          state_shortconv[l], state_gla, state_ffn_conv[l], ks_all, vs_all)
        for i in range(4):
            outs_p[i].append(st_p[i])
            outs_s[i].append(st_s[i])
    stacked_p = [jnp.stack(o) for o in outs_p]
    stacked_s = [jnp.stack(o) for o in outs_s]
    kv_p = lambda a: a.reshape(depth, bp, heads, hd, sp).transpose(0, 1, 4, 2, 3)
    kv_s = lambda a: a.reshape(depth, bs, ss, heads, hd)
    return (xp.reshape(bp, sp, d), xs.reshape(bs, ss, d),
            kv_p(kp_all), kv_p(vp_all), *stacked_p,
            kv_s(ks_all), kv_s(vs_all), *stacked_s)
```

```python
import functools

import jax
import jax.numpy as jnp
from jax import lax
from jax.experimental import pallas as pl
from jax.experimental.pallas import tpu as pltpu

F32 = jnp.float32
BF16 = jnp.bfloat16
LN_EPS = 1e-5
GLA_TAU = 16.0
LANES = 128
SUBLANES = 8
VMEM_LIMIT = 56 * 1024 * 1024
HIGHEST = lax.Precision.HIGHEST
LOG2E = 1.4426950408889634
FOX_UNDERFLOW = 160.0
FOX_NORM_MARGIN = 1.01

TM_PROJ = 512
TM_MERGE = 512
TM_FFN = 512
TQ_FOX = 512
T_GLA = 512
GLA_CHUNK = 128
FFN_CHUNK = 256
ADA_TN = 1024

_NT = (((1,), (1,)), ((), ()))
_TN = (((0,), (0,)), ((), ()))


def _tile(n, pref):
    if n <= pref:
        return n
    t = pref
    while n % t:
        t //= 2
    return t


def _params(*sem):
    return pltpu.CompilerParams(dimension_semantics=sem, vmem_limit_bytes=VMEM_LIMIT)


def _resident(block, index_map):
    return pl.BlockSpec(block, index_map, pipeline_mode=pl.Buffered(1))


def _log_sigmoid(x):
    return jnp.minimum(x, 0.0) - jnp.log1p(jnp.exp(-jnp.abs(x)))


def _ln(x):
    mu = jnp.mean(x, axis=-1, keepdims=True)
    xc = x - mu
    var = jnp.mean(xc * xc, axis=-1, keepdims=True)
    return xc * lax.rsqrt(var + LN_EPS)


def _causal_conv3(u, p0, p1, w_ref, b_ref, cols):
    tm = u.shape[0]
    w0, w1, w2, bias = w_ref[0:1, cols], w_ref[1:2, cols], w_ref[2:3, cols], b_ref[:, cols]

    def taps(u0, u1, u2):
        y = bias + u2 * w0
        y = y + u1 * w1
        return y + u0 * w2

    head = u[0:SUBLANES, :]
    row = lax.broadcasted_iota(jnp.int32, (SUBLANES, 1), 0)
    h1 = jnp.where(row == 0, p1, pltpu.roll(head, 1, 0))
    h2 = jnp.where(row == 0, p0, jnp.where(row == 1, p1, pltpu.roll(head, 2, 0)))
    first = taps(head, h1, h2)
    if tm == SUBLANES:
        return first
    rest = taps(u, pltpu.roll(u, 1, 0), pltpu.roll(u, 2, 0))
    return jnp.concatenate([first, rest[SUBLANES:, :]], axis=0)


def _ada_kernel(c_ref, w_ref, b_ref, o_ref):
    c = c_ref[...]
    s = (c * jax.nn.sigmoid(c)).astype(BF16)
    o_ref[...] = jnp.dot(s, w_ref[...].astype(BF16), preferred_element_type=F32) + b_ref[...]


def _ada(c_all, w_ada, b_ada):
    depth, d, n6 = w_ada.shape
    r = c_all.shape[0]
    tn = _tile(n6, ADA_TN)
    return pl.pallas_call(
        _ada_kernel,
        grid=(depth, n6 // tn),
        in_specs=[pl.BlockSpec((r, d), lambda l, n: (0, 0)),
                  pl.BlockSpec((None, d, tn), lambda l, n: (l, 0, n)),
                  pl.BlockSpec((None, 1, tn), lambda l, n: (l, 0, n))],
        out_specs=pl.BlockSpec((None, r, tn), lambda l, n: (l, 0, n)),
        out_shape=jax.ShapeDtypeStruct((depth, r, n6), F32),
        compiler_params=_params("arbitrary", "arbitrary"),
        name="ada_mod",
    )(c_all, w_ada, b_ada.reshape(depth, 1, n6))


def _inproj_kernel(x_ref, mod_ref, w_ref, wff_ref, wglr_ref, bff_ref, wup_ref, ba_ref, scw_ref, scb_ref,
                   prev_ref, kf_all_ref, vf_all_ref,
                   q_ref, kf_ref, vf_ref, kb_ref, vb_ref, lf_ref, ysc_ref, scst_ref,
                   gq_ref, gk_ref, gv_ref, gg_ref, la_ref,
                   *tail, d, fw, hd, q_scale, gq_scale, transposed):
    del kf_all_ref, vf_all_ref
    carry_ref = tail[-1]
    j = pl.program_id(1)
    tm = x_ref.shape[0]
    h = _ln(x_ref[...]) * (1.0 + mod_ref[:, d:2 * d]) + mod_ref[:, 0:d]
    hb = h.astype(BF16)

    def proj(g):
        return jnp.dot(hb, w_ref[:, g * fw:(g + 1) * fw], preferred_element_type=F32)

    q = proj(0) * q_scale
    k = proj(1)
    v = proj(2)
    kb_ref[...] = k.astype(BF16)
    if transposed:
        vt = v.T
        q_ref[...] = q.T.astype(BF16)
        kf_ref[...] = k.T
        vf_ref[...] = vt
        vb_ref[...] = vt.astype(BF16)
        kk = k * k
        lane = lax.broadcasted_iota(jnp.int32, (1, LANES), 1)
        norms = jnp.zeros((1, LANES), F32)
        for hh in range(fw // hd):
            n2 = jnp.sum(kk[:, hh * hd:(hh + 1) * hd], axis=-1, keepdims=True)
            norms = jnp.where(lane == hh, jnp.max(n2, axis=0, keepdims=True), norms)
        tail[0][...] = norms
    else:
        q_ref[...] = q.astype(BF16)
        kf_ref[...] = k
        vf_ref[...] = v
        vb_ref[...] = v.astype(BF16)
    fft =lax.dot_general(wff_ref[...], hb, _NT, preferred_element_type=F32)
    lf_ref[...] = _log_sigmoid(fft + bff_ref[...])

    first = j == 0
    p0 = jnp.where(first, prev_ref[0:1, :], carry_ref[SUBLANES - 2:SUBLANES - 1, :])
    p1 = jnp.where(first, prev_ref[1:2, :], carry_ref[SUBLANES - 1:SUBLANES, :])
    sb = proj(3)
    u = proj(4) * proj(5)
    y = _causal_conv3(u, p0, p1, scw_ref, scb_ref, slice(None))
    ysc_ref[...] = (sb * y).astype(BF16)
    carry_ref[...] = u[tm - SUBLANES:tm, :]
    scst_ref[...] = carry_ref[SUBLANES - 2:SUBLANES, :]

    gq_ref[...] = (proj(6) * gq_scale).astype(BF16)
    gk_ref[...] = proj(7).astype(BF16)
    gv_ref[...] = proj(8).astype(BF16)
    gg = proj(9)
    gg_ref[...] = (gg * jax.nn.sigmoid(gg)).astype(BF16)
    glr = jnp.dot(hb, wglr_ref[...], preferred_element_type=F32)
    lap = jnp.dot(glr.astype(BF16), wup_ref[...], preferred_element_type=F32) + ba_ref[...]
    la_ref[...] = _log_sigmoid(lap) / GLA_TAU


def _inproj(l, x2d, mod, wp, sc_prev, kf_all, vf_all, nb, seq, hd, dk, transposed):
    n, d = x2d.shape
    fw = wp["fw"]
    heads = fw // hd
    tm = _tile(seq, TM_PROJ)
    nt = seq // tm
    rows = lambda b, j: (b * nt + j, 0)
    wl = lambda b, j: (l, 0, 0)
    act = lambda dt: jax.ShapeDtypeStruct((n, fw), dt)
    act_spec = pl.BlockSpec((tm, fw), rows)
    if transposed:
        t_shape, t_spec = jax.ShapeDtypeStruct((fw, n), BF16), pl.BlockSpec((fw, tm), lambda b, j: (0, b * nt + j))
        q_scale = LOG2E * hd ** -0.5
        stacked = pl.BlockSpec((None, None, fw, tm), lambda b, j: (l, b, 0, j))
    else:
        t_shape, t_spec, q_scale = act(BF16), act_spec, hd ** -0.5
        stacked = pl.BlockSpec((None, tm, fw), lambda b, j: (l, b * nt + j, 0))
    hbm = pl.BlockSpec(memory_space=pl.ANY)
    kern = functools.partial(_inproj_kernel, d=d, fw=fw, hd=hd, q_scale=q_scale, gq_scale=dk ** -0.5,
                             transposed=transposed)
    extra_specs = [pl.BlockSpec((None, 1, LANES), lambda b, j: (b * nt + j, 0, 0))] if transposed else []
    extra_shapes = [jax.ShapeDtypeStruct((nb * nt, 1, LANES), F32)] if transposed else []
    return pl.pallas_call(
        kern,
        grid=(nb, nt),
        in_specs=[pl.BlockSpec((tm, d), rows),
                  pl.BlockSpec((None, 1, 6 * d), lambda b, j: (b, 0, 0)),
                  _resident((None, d, 10 * fw), wl),
                  _resident((None, heads, d), wl),
                  _resident((None, d, LANES), wl),
                  _resident((None, heads, 1), wl),
                  _resident((None, LANES, fw), wl),
                  _resident((None, 1, fw), wl),
                  _resident((None, 3, fw), wl),
                  _resident((None, 1, fw), wl),
                  pl.BlockSpec((None, 2, fw), lambda b, j: (b, 0, 0)),
                  hbm, hbm],
        out_specs=[t_spec, stacked, stacked, act_spec, t_spec,
                   pl.BlockSpec((None, heads, tm), lambda b, j: (b * nt + j, 0, 0)),
                   act_spec,
                   pl.BlockSpec((None, 2, fw), lambda b, j: (b, 0, 0)),
                   act_spec, act_spec, act_spec, act_spec, act_spec] + extra_specs,
        out_shape=[t_shape, jax.ShapeDtypeStruct(kf_all.shape, F32), jax.ShapeDtypeStruct(vf_all.shape, F32),
                   act(BF16), t_shape,
                   jax.ShapeDtypeStruct((nb * nt, heads, tm), F32),
                   act(BF16),
                   jax.ShapeDtypeStruct((nb, 2, fw), F32),
                   act(BF16), act(BF16), act(BF16), act(BF16), act(F32)] + extra_shapes,
        input_output_aliases={11: 1, 12: 2},
        scratch_shapes=[pltpu.VMEM((SUBLANES, fw), F32)],
        compiler_params=_params("arbitrary", "arbitrary"),
        name="in_proj",
    )(x2d, mod, wp["w_main"], wp["w_fft"], wp["w_glr"], wp["b_fox"], wp["w_gla_up"], wp["b_gla"],
      wp["sc_w"], wp["sc_b"], sc_prev, kf_all, vf_all)


def _lane_cumsum(x, lane):
    for k in range(7):
        s = 1 << k
        x = x + jnp.where(lane >= s, pltpu.roll(x, s, 1), 0.0)
    return x


def _cumsum_kernel(x_ref, cp_ref, c1_ref, c2_ref, c3_ref):
    r, length = x_ref.shape
    lane = lax.broadcasted_iota(jnp.int32, (r, LANES), 1)

    def body(i, carry):
        off = pl.multiple_of(i * LANES, LANES)
        x = _lane_cumsum(x_ref[:, pl.ds(off, LANES)], lane) + carry
        c = x * LOG2E
        c1 = c.astype(BF16).astype(F32)
        r1 = c - c1
        c2 = r1.astype(BF16).astype(F32)
        c3 = (r1 - c2).astype(BF16).astype(F32)
        cp_ref[:, pl.ds(off, LANES)] = c
        c1_ref[:, pl.ds(off, LANES)] = c1
        c2_ref[:, pl.ds(off, LANES)] = c2
        c3_ref[:, pl.ds(off, LANES)] = c3
        return x[:, LANES - 1:LANES]

    lax.fori_loop(0, length // LANES, body, jnp.zeros((r, 1), F32), unroll=min(8, length // LANES))


def _cumsum_rows(x):
    r, length = x.shape
    spec = pl.BlockSpec((r, length), lambda i: (0, 0))
    return pl.pallas_call(
        _cumsum_kernel,
        grid=(1,),
        in_specs=[spec],
        out_specs=[spec, spec, spec, spec],
        out_shape=[jax.ShapeDtypeStruct((r, length), F32)] * 4,
        compiler_params=_params("arbitrary"),
        name="fox_cumsum",
    )(x)


def _fox_kernel(cend_ref, qt_ref, cq1_ref, cq2_ref, cq3_ref, k_ref, ck1_ref, ck2_ref, ck3_ref, vt_ref, o_ref,
                m_ref, acc_ref, qs_ref, ak_ref, *, hd):
    b = pl.program_id(0)
    pr = pl.program_id(1)
    i = pl.program_id(2)
    tq = qt_ref.shape[1]
    seq = k_ref.shape[0]
    nk = seq // tq
    lane = lax.broadcasted_iota(jnp.int32, (1, LANES), 1)
    head_a = lane < hd
    cqs = (cq1_ref, cq2_ref, cq3_ref)
    cks = (ck1_ref, ck2_ref, ck3_ref)

    group = 2 * SUBLANES

    @pl.when(i == 0)
    def _():
        rk = lax.broadcasted_iota(jnp.int32, (group, LANES), 0)
        gap = jnp.zeros((hd - group, LANES), F32)

        def chunk(t, carry):
            off = pl.multiple_of(t * LANES, LANES)
            parts = []
            for h in (1, 0):
                rows = jnp.where(rk < 3, 1.0, 0.0)
                for n, ck in enumerate(cks):
                    rows = jnp.where(rk == 3 + n, -ck[h, :, pl.ds(off, LANES)], rows)
                parts += [rows, gap]
            ak_ref[pl.ds(off, LANES), :] = jnp.concatenate(parts, axis=0).T.astype(BF16)
            return carry

        lax.fori_loop(0, seq // LANES, chunk, 0)

    rg = lax.broadcasted_iota(jnp.int32, (group, tq), 0)

    def bias_rows(h):
        rows = jnp.where(rg < 6, 1.0, 0.0)
        for n, cq in enumerate(cqs):
            rows = jnp.where(rg == n, cq[h], rows)
        return rows.astype(BF16)

    qs_ref[0, 0:hd, :] = qt_ref[0:hd, :]
    qs_ref[0, hd:hd + group, :] = bias_rows(0)
    qs_ref[0, hd + group:, :] = jnp.zeros((hd - group, tq), BF16)
    qs_ref[1, 0:group, :] = bias_rows(1)
    qs_ref[1, group:hd, :] = jnp.zeros((hd - group, tq), BF16)
    qs_ref[1, hd:, :] = qt_ref[hd:, :]
    m_ref[...] = jnp.full(m_ref.shape, -jnp.inf, F32)
    acc_ref[...] = jnp.zeros(acc_ref.shape, F32)

    def step(jk, tk, diagonal):
        off = pl.multiple_of(jk * tq, tq)
        ones = jnp.ones((hd, tk), BF16)
        k = k_ref[pl.ds(off, tk), :]
        ak = ak_ref[pl.ds(off, tk), :]
        ks = (jnp.where(head_a, k, ak), jnp.where(head_a, ak, k))
        vts = (jnp.concatenate([vt_ref[0:hd, pl.ds(off, tk)], ones], axis=0),
               jnp.concatenate([ones, vt_ref[hd:, pl.ds(off, tk)]], axis=0))
        sts = [jnp.dot(ks[h], qs_ref[h], preferred_element_type=F32) for h in range(2)]
        if diagonal:
            krow = lax.broadcasted_iota(jnp.int32, (tk, tq), 0)
            qcol = lax.broadcasted_iota(jnp.int32, (tk, tq), 1)
            sts = [jnp.where(krow <= qcol, st, -jnp.inf) for st in sts]
        m_prevs = [m_ref[h] for h in range(2)]
        m_news = [jnp.maximum(m_prevs[h], jnp.max(sts[h], axis=0, keepdims=True)) for h in range(2)]
        for h in range(2):
            p = jnp.exp2(sts[h] - m_news[h]).astype(BF16)
            alpha = jnp.exp2(m_prevs[h] - m_news[h])
            acc_ref[h] = alpha * acc_ref[h] + jnp.dot(vts[h], p, preferred_element_type=F32)
            m_ref[h] = m_news[h]

    step(i, tq, True)

    pair = b * pl.num_programs(1) + pr
    base = pair * 2 * nk
    kmax_base = pl.num_programs(0) * pl.num_programs(1) * 2 * nk + pair * 2
    slack = []
    for h in range(2):
        qf = qt_ref[h * hd:(h + 1) * hd, :].astype(F32)
        qn = jnp.sqrt(jnp.max(jnp.sum(qf * qf, axis=0, keepdims=True)))
        slack.append(qn * cend_ref[kmax_base + h] * FOX_NORM_MARGIN - jnp.min(m_ref[h]))
    before = jnp.maximum(i - 1, 0)

    def needed(j):
        r = False
        for h in range(2):
            gap = cend_ref[base + h * nk + before] - cend_ref[base + h * nk + j]
            r = jnp.logical_or(r, slack[h] + gap > -FOX_UNDERFLOW)
        return r

    j_lo = lax.while_loop(lambda j: jnp.logical_and(j > 0, needed(jnp.maximum(j - 1, 0))), lambda j: j - 1, i)
    n_off = i - j_lo

    def body(t, carry):
        step(j_lo + 2 * t, 2 * tq, False)
        return carry

    lax.fori_loop(0, n_off // 2, body, 0)

    @pl.when(n_off % 2 == 1)
    def _():
        step(i - 1, tq, False)

    acc_a = acc_ref[0]
    acc_b = acc_ref[1]
    out_t = jnp.concatenate([acc_a[0:hd] / acc_a[hd:hd + 1], acc_b[hd:] / acc_b[0:1]], axis=0)
    o_ref[...] = out_t.T.astype(BF16)


def _fox_prompt(cp, c1, c2, c3, knorm, qt, kb, vt, nb, seq, hd):
    fw, n = qt.shape
    pairs = fw // LANES
    tq = _tile(seq, TQ_FOX)
    nq = seq // tq
    by_pair = lambda a: a.reshape(pairs, 2, nb, 1, seq)
    cend = by_pair(cp)[:, :, :, 0, tq - 1::tq].transpose(2, 0, 1, 3).reshape(-1)
    cend = jnp.concatenate([cend, knorm.reshape(-1)])
    terms = [by_pair(c) for c in (c1, c2, c3)]
    qmap = lambda b, p, i, c: (p, b * nq + i)
    kmap = lambda b, p, i, c: (b, p)
    cq_spec = pl.BlockSpec((None, 2, None, 1, tq), lambda b, p, i, c: (p, 0, b, 0, i))
    ck_spec = pl.BlockSpec((None, 2, None, 1, seq), lambda b, p, i, c: (p, 0, b, 0, 0))
    grid_spec = pltpu.PrefetchScalarGridSpec(
        num_scalar_prefetch=1,
        grid=(nb, pairs, nq),
        in_specs=[pl.BlockSpec((LANES, tq), qmap), cq_spec, cq_spec, cq_spec,
                  pl.BlockSpec((seq, LANES), kmap), ck_spec, ck_spec, ck_spec,
                  pl.BlockSpec((LANES, seq), lambda b, p, i, c: (p, b))],
        out_specs=pl.BlockSpec((tq, LANES), lambda b, p, i, c: (b * nq + i, p)),
        scratch_shapes=[pltpu.VMEM((2, 1, tq), F32), pltpu.VMEM((2, LANES, tq), F32),
                        pltpu.VMEM((2, LANES, tq), BF16), pltpu.VMEM((seq, LANES), BF16)],
    )
    return pl.pallas_call(
        functools.partial(_fox_kernel, hd=hd),
        grid_spec=grid_spec,
        out_shape=jax.ShapeDtypeStruct((n, fw), BF16),
        compiler_params=_params("arbitrary", "arbitrary", "arbitrary"),
        name="fox_attn",
    )(cend, qt, *terms, kb, *terms, vt)


def _fox_cached_kernel(q_ref, kn_ref, vn_ref, kc_ref, vc_ref, lcn_ref, lct_ref, lnn_ref, lnt_ref, o_ref,
                       ck_ref, *, hd):
    ln, fw = q_ref.shape
    past = kc_ref.shape[1]
    heads = fw // hd
    lane8 = lax.broadcasted_iota(jnp.int32, (heads, LANES), 1)

    def body(i, carry):
        off = pl.multiple_of(i * LANES, LANES)
        x = _lane_cumsum(lct_ref[:, pl.ds(off, LANES)], lane8) + carry
        ck_ref[:, pl.ds(off, LANES)] = x
        return x[:, LANES - 1:LANES]

    c_last = lax.fori_loop(0, past // LANES, body, jnp.zeros((heads, 1), F32),
                           unroll=min(8, past // LANES))
    r_i = lax.broadcasted_iota(jnp.int32, (ln, ln), 0)
    c_i = lax.broadcasted_iota(jnp.int32, (ln, ln), 1)
    causal = c_i <= r_i
    tril = causal.astype(F32)
    triu = (r_i <= c_i).astype(F32)
    ck_new = c_last + jnp.dot(lnt_ref[...], triu, precision=HIGHEST, preferred_element_type=F32)
    cache_total = jnp.dot(jnp.ones((ln, past), F32), lcn_ref[...], precision=HIGHEST,
                          preferred_element_type=F32)
    cq = cache_total + jnp.dot(tril, lnn_ref[...], precision=HIGHEST, preferred_element_type=F32)

    outs = []
    for h in range(heads):
        feat = slice(h * hd, (h + 1) * hd)
        qh = q_ref[:, feat]
        kct = kc_ref[feat, :].astype(BF16)
        vct = vc_ref[feat, :].astype(BF16)
        cqh = cq[:, h:h + 1]
        s_c = jnp.dot(qh, kct, preferred_element_type=F32) + cqh - ck_ref[h:h + 1, :]
        s_n = lax.dot_general(qh, kn_ref[:, feat], _NT, preferred_element_type=F32) + cqh - ck_new[h:h + 1, :]
        s_n = jnp.where(causal, s_n, -jnp.inf)
        m = jnp.maximum(jnp.max(s_c, axis=-1, keepdims=True), jnp.max(s_n, axis=-1, keepdims=True))
        p_c = jnp.exp(s_c - m)
        p_n = jnp.exp(s_n - m)
        den = jnp.sum(p_c, axis=-1, keepdims=True) + jnp.sum(p_n, axis=-1, keepdims=True)
        acc = (lax.dot_general(p_c.astype(BF16), vct, _NT, preferred_element_type=F32)
               + jnp.dot(p_n.astype(BF16), vn_ref[:, feat], preferred_element_type=F32))
        outs.append(acc / den)
    o_ref[...] = jnp.concatenate(outs, axis=1).astype(BF16)


def _fox_cached(l, qb, kb, vb, cache_k, cache_v, lf_cache_nat, lf_cache_t, lf_new_nat, lf_new_t, nb, ln, hd):
    n, fw = qb.shape
    past = cache_k.shape[3]
    heads = fw // hd
    rows = lambda b: (b, 0)
    cache = lambda b: (l, b, 0, 0)
    act = pl.BlockSpec((ln, fw), rows)
    return pl.pallas_call(
        functools.partial(_fox_cached_kernel, hd=hd),
        grid=(nb,),
        in_specs=[act, act, act,
                  pl.BlockSpec((None, None, fw, past), cache),
                  pl.BlockSpec((None, None, fw, past), cache),
                  pl.BlockSpec((None, None, past, heads), cache),
                  pl.BlockSpec((None, None, heads, past), cache),
                  pl.BlockSpec((None, ln, heads), lambda b: (b, 0, 0)),
                  pl.BlockSpec((None, heads, ln), lambda b: (b, 0, 0))],
        out_specs=act,
        out_shape=jax.ShapeDtypeStruct((n, fw), BF16),
        scratch_shapes=[pltpu.VMEM((heads, past), F32)],
        compiler_params=_params("arbitrary"),
        name="fox_attn_cached",
    )(qb, kb, vb, cache_k, cache_v, lf_cache_nat, lf_cache_t, lf_new_nat, lf_new_t)


def _gla_kernel(q_ref, k_ref, v_ref, g_ref, la_ref, s0_ref, gn_ref, y_ref, sout_ref, st_ref,
                *, chunk, nh, dk, dv):
    j = pl.program_id(1)
    t_rows = q_ref.shape[0]

    @pl.when(j == 0)
    def _():
        for h in range(nh):
            st_ref[h] = s0_ref[h].T

    row = lax.broadcasted_iota(jnp.int32, (chunk, chunk), 0)
    col = lax.broadcasted_iota(jnp.int32, (chunk, chunk), 1)
    causal = col <= row
    tril = causal.astype(F32)
    mid = chunk // 2
    for c in range(t_rows // chunk):
        rows = slice(c * chunk, (c + 1) * chunk)
        for h in range(nh):
            kc = slice(h * dk, (h + 1) * dk)
            vc = slice(h * dv, (h + 1) * dv)
            b = jnp.dot(tril, la_ref[rows, kc], precision=HIGHEST, preferred_element_type=F32)
            b_last = b[chunk - 1:chunk, :]
            b_mid = b[mid:mid + 1, :]
            q = q_ref[rows, kc].astype(F32)
            k = k_ref[rows, kc].astype(F32)
            v = v_ref[rows, vc]
            qi = (q * jnp.exp(b - b_mid)).astype(BF16)
            ki = (k * jnp.exp(b_mid - b)).astype(BF16)
            a = lax.dot_general(qi, ki, _NT, preferred_element_type=F32)
            a = jnp.where(causal, a, 0.0)
            o = jnp.dot(a.astype(BF16), v, preferred_element_type=F32)
            st = st_ref[h]
            qe = (q * jnp.exp(b)).astype(BF16)
            o = o + lax.dot_general(qe, st.astype(BF16), _NT, preferred_element_type=F32)
            kd = (k * jnp.exp(b_last - b)).astype(BF16)
            st_ref[h] = st * jnp.exp(b_last) + lax.dot_general(v, kd, _TN, preferred_element_type=F32)
            ms = jnp.mean(o * o, axis=-1, keepdims=True)
            y = o * lax.rsqrt(ms + LN_EPS) * gn_ref[...]
            y_ref[rows, vc] = (y * g_ref[rows, vc].astype(F32)).astype(BF16)

    @pl.when(j == pl.num_programs(1) - 1)
    def _():
        for h in range(nh):
            sout_ref[h] = st_ref[h].T


def _gla(l, gq, gk, gv, gg, la, s0, gn, nb, seq, s0_layered):
    n, kw = gq.shape
    nh, dk, dv = s0.shape[-3:]
    t_rows = _tile(seq, T_GLA)
    chunk = _tile(t_rows, GLA_CHUNK)
    nt = seq // t_rows
    rows = lambda b, j: (b * nt + j, 0)
    act = pl.BlockSpec((t_rows, kw), rows)
    if s0_layered:
        s0_spec = pl.BlockSpec((None, None, nh, dk, dv), lambda b, j: (l, b, 0, 0, 0))
    else:
        s0_spec = pl.BlockSpec((None, nh, dk, dv), lambda b, j: (b, 0, 0, 0))
    return pl.pallas_call(
        functools.partial(_gla_kernel, chunk=chunk, nh=nh, dk=dk, dv=dv),
        grid=(nb, nt),
        in_specs=[act, act, act, act, act, s0_spec,
                  _resident((None, 1, dv), lambda b, j: (l, 0, 0))],
        out_specs=[act, pl.BlockSpec((None, nh, dk, dv), lambda b, j: (b, 0, 0, 0))],
        out_shape=[jax.ShapeDtypeStruct((n, kw), BF16), jax.ShapeDtypeStruct((nb, nh, dk, dv), F32)],
        scratch_shapes=[pltpu.VMEM((nh, dv, dk), F32)],
        compiler_params=_params("arbitrary", "arbitrary"),
        name="gla",
    )(gq, gk, gv, gg, la, s0, gn)


def _merge_kernel(x_ref, mod_ref, yf_ref, ys_ref, yg_ref, wm_ref, wb_ref, wo_ref, g_ref, b_ref, o_ref,
                  *, d, alpha):
    x = x_ref[...]
    hb = (_ln(x) * (1.0 + mod_ref[:, d:2 * d]) + mod_ref[:, 0:d]).astype(BF16)
    merged = None
    for i, y_ref in enumerate((yf_ref, ys_ref, yg_ref)):
        gate = jax.nn.sigmoid(jnp.dot(hb, wm_ref[:, i * d:(i + 1) * d], preferred_element_type=F32))
        term = gate * jnp.dot(y_ref[...], wb_ref[i], preferred_element_type=F32)
        merged = term if merged is None else merged + term
    mix = jnp.dot(merged.astype(BF16), wo_ref[...], preferred_element_type=F32)
    z = alpha * x + mod_ref[:, 2 * d:3 * d] * mix
    o_ref[...] = _ln(z) * g_ref[...] + b_ref[...]


def _merge(l, x2d, mod, y_fox, y_sc, y_gla, wp, nb, seq, alpha):
    n, d = x2d.shape
    bw = y_fox.shape[1]
    tm = _tile(seq, TM_MERGE)
    nt = seq // tm
    rows = lambda b, j: (b * nt + j, 0)
    wl3 = lambda b, j: (l, 0, 0)
    yspec = pl.BlockSpec((tm, bw), rows)
    return pl.pallas_call(
        functools.partial(_merge_kernel, d=d, alpha=alpha),
        grid=(nb, nt),
        in_specs=[pl.BlockSpec((tm, d), rows),
                  pl.BlockSpec((None, 1, 6 * d), lambda b, j: (b, 0, 0)),
                  yspec, yspec, yspec,
                  _resident((None, d, 3 * d), wl3),
                  _resident((None, 3, bw, d), lambda b, j: (l, 0, 0, 0)),
                  _resident((None, d, d), wl3),
                  _resident((None, 1, d), wl3),
                  _resident((None, 1, d), wl3)],
        out_specs=pl.BlockSpec((tm, d), rows),
        out_shape=jax.ShapeDtypeStruct((n, d), F32),
        compiler_params=_params("arbitrary", "arbitrary"),
        name="merge_ln1",
    )(x2d, mod, y_fox, y_sc, y_gla, wp["w_merge"], wp["w_branch"], wp["w_out"], wp["ln1_g"], wp["ln1_b"])


def _ffn_kernel(x_ref, mod_ref, wu_ref, cw_ref, cb_ref, wd_ref, g_ref, b_ref, prev_ref, o_ref, st_ref,
                carry_ref, act_ref, *, d, dff, chunk, alpha):
    j = pl.program_id(1)
    tm = x_ref.shape[0]
    first = j == 0
    x = x_ref[...]
    hb = (_ln(x) * (1.0 + mod_ref[:, 4 * d:5 * d]) + mod_ref[:, 3 * d:4 * d]).astype(BF16)
    for c0 in range(0, dff, chunk):
        cols = slice(c0, c0 + chunk)
        ug = jnp.dot(hb, wu_ref[:, cols], preferred_element_type=F32)
        uv = jnp.dot(hb, wu_ref[:, dff + c0:dff + c0 + chunk], preferred_element_type=F32)
        p0 = jnp.where(first, prev_ref[0:1, cols], carry_ref[SUBLANES - 2:SUBLANES - 1, cols])
        p1 = jnp.where(first, prev_ref[1:2, cols], carry_ref[SUBLANES - 1:SUBLANES, cols])
        ugc = _causal_conv3(ug, p0, p1, cw_ref, cb_ref, cols)
        carry_ref[:, cols] = ug[tm - SUBLANES:tm, :]
        act_ref[:, cols] = (jax.nn.gelu(ugc) * uv).astype(BF16)
    f = jnp.dot(act_ref[...], wd_ref[...], preferred_element_type=F32)
    st_ref[...] = carry_ref[SUBLANES - 2:SUBLANES, :]
    z = alpha * x + mod_ref[:, 5 * d:6 * d] * f
    o_ref[...] = _ln(z) * g_ref[...] + b_ref[...]


def _ffn(l, x2d, mod, wp, prev, nb, seq, alpha):
    n, d = x2d.shape
    dff = wp["w_down"].shape[1]
    tm = _tile(seq, TM_FFN)
    nt = seq // tm
    chunk = _tile(dff, FFN_CHUNK)
    rows = lambda b, j: (b * nt + j, 0)
    wl3 = lambda b, j: (l, 0, 0)
    st = pl.BlockSpec((None, 2, dff), lambda b, j: (b, 0, 0))
    return pl.pallas_call(
        functools.partial(_ffn_kernel, d=d, dff=dff, chunk=chunk, alpha=alpha),
        grid=(nb, nt),
        in_specs=[pl.BlockSpec((tm, d), rows),
                  pl.BlockSpec((None, 1, 6 * d), lambda b, j: (b, 0, 0)),
                  _resident((None, d, 2 * dff), wl3),
                  _resident((None, 3, dff), wl3),
                  _resident((None, 1, dff), wl3),
                  _resident((None, dff, d), wl3),
                  _resident((None, 1, d), wl3),
                  _resident((None, 1, d), wl3),
                  st],
        out_specs=[pl.BlockSpec((tm, d), rows), st],
        out_shape=[jax.ShapeDtypeStruct((n, d), F32), jax.ShapeDtypeStruct((nb, 2, dff), F32)],
        scratch_shapes=[pltpu.VMEM((SUBLANES, dff), F32), pltpu.VMEM((tm, dff), BF16)],
        compiler_params=_params("arbitrary", "arbitrary"),
        name="conv_ffn_ln2",
    )(x2d, mod, wp["w_up"], wp["ffn_conv_w"], wp["ffn_conv_b"], wp["w_down"], wp["ln2_g"], wp["ln2_b"], prev)


def _pack_weights(w_in, b_fox_f, w_gla_up, b_gla_a, gla_norm_g, sc_conv_w, sc_conv_b, w_branch, w_out,
                  ln1_g, ln1_b, w_up, ffn_conv_w, ffn_conv_b, w_down, ln2_g, ln2_b, heads):
    depth, d, _ = w_in.shape
    fw = sc_conv_w.shape[-1]
    rank = w_gla_up.shape[1]
    o = 0
    fox = w_in[:, :, o:o + 3 * fw]; o += 3 * fw
    wff = w_in[:, :, o:o + heads]; o += heads
    sc = w_in[:, :, o:o + 3 * fw]; o += 3 * fw
    gla = w_in[:, :, o:o + 4 * fw]; o += 4 * fw
    glr = w_in[:, :, o:o + rank]; o += rank
    wm = w_in[:, :, o:]
    row = lambda a: a.reshape(depth, 1, a.shape[-1])
    return {
        "fw": fw,
        "w_main": jnp.concatenate([fox, sc, gla], axis=-1).astype(BF16),
        "w_fft": jnp.swapaxes(wff, 1, 2).astype(BF16),
        "w_glr": jnp.pad(glr, ((0, 0), (0, 0), (0, LANES - rank))).astype(BF16),
        "b_fox": b_fox_f.reshape(depth, heads, 1),
        "w_gla_up": jnp.pad(w_gla_up, ((0, 0), (0, LANES - rank), (0, 0))).astype(BF16),
        "b_gla": row(b_gla_a),
        "gla_norm_g": row(gla_norm_g),
        "sc_w": sc_conv_w, "sc_b": row(sc_conv_b),
        "w_merge": wm.astype(BF16),
        "w_branch": w_branch.astype(BF16),
        "w_out": w_out.astype(BF16),
        "ln1_g": row(ln1_g), "ln1_b": row(ln1_b),
        "w_up": w_up.astype(BF16),
        "ffn_conv_w": ffn_conv_w, "ffn_conv_b": row(ffn_conv_b),
        "w_down": w_down.astype(BF16),
        "ln2_g": row(ln2_g), "ln2_b": row(ln2_b),
    }


def _layer(l, x2d, mod, wp, nb, seq, hd, alpha, fox_cache, sc_prev, gla_s0, ffn_prev, kf_all, vf_all):
    fw = wp["fw"]
    heads = fw // hd
    dk = gla_s0.shape[-2]
    prompt = fox_cache is None
    (q, kf_all, vf_all, kb, v, lft, y_sc, sc_new, gq, gk, gv, gg, la, *kn2) = _inproj(
        l, x2d, mod, wp, sc_prev, kf_all, vf_all, nb, seq, hd, dk, prompt)
    lf_t = lft.reshape(nb, -1, heads, lft.shape[-1]).transpose(2, 0, 1, 3).reshape(heads, nb, seq)
    logf = lf_t.transpose(1, 2, 0)
    if prompt:
        cp, c1, c2, c3 = _cumsum_rows(lf_t.reshape(heads * nb, seq))
        knorm = jnp.sqrt(jnp.max(kn2[0][:, 0, :heads].reshape(nb, -1, heads), axis=1))
        y_fox = _fox_prompt(cp, c1, c2, c3, knorm, q, kb, v, nb, seq, hd)
        y_gla, gla_new = _gla(l, gq, gk, gv, gg, la, gla_s0, wp["gla_norm_g"], nb, seq, False)
    else:
        cache_k, cache_v, lf_cache_nat, lf_cache_t = fox_cache
        y_fox = _fox_cached(l, q, kb, v, cache_k, cache_v, lf_cache_nat, lf_cache_t,
                            logf, lf_t.transpose(1, 0, 2), nb, seq, hd)
        y_gla, gla_new = _gla(l, gq, gk, gv, gg, la, gla_s0, wp["gla_norm_g"], nb, seq, True)
    x1 = _merge(l, x2d, mod, y_fox, y_sc, y_gla, wp, nb, seq, alpha)
    x2, ffn_new = _ffn(l, x1, mod, wp, ffn_prev, nb, seq, alpha)
    return x2, kf_all, vf_all, (logf, sc_new, gla_new, ffn_new)


def kernel(x_prompt, x_sample, c_prompt, c_sample, cache_fox_k, cache_fox_v, cache_fox_logf, state_shortconv, state_gla, state_ffn_conv, w_ada, b_ada, w_in, b_fox_f, w_gla_up, b_gla_a, gla_norm_g, sc_conv_w, sc_conv_b, w_branch, w_out, ln1_g, ln1_b, w_up, ffn_conv_w, ffn_conv_b, w_down, ln2_g, ln2_b):
    bp, sp, d = x_prompt.shape
    bs, ss, _ = x_sample.shape
    depth = w_in.shape[0]
    _, _, past, heads, hd = cache_fox_k.shape
    fw = heads * hd
    dff = ffn_conv_w.shape[-1]
    assert sc_conv_w.shape[-1] == fw and state_gla.shape[2] * state_gla.shape[3] == fw
    assert state_gla.shape[2] * state_gla.shape[4] == fw and fw % LANES == 0 and 2 * hd == LANES
    assert sc_conv_w.shape[1] == 3 and ffn_conv_w.shape[1] == 3
    alpha = (2.0 * depth) ** 0.25

    wp = _pack_weights(w_in, b_fox_f, w_gla_up, b_gla_a, gla_norm_g, sc_conv_w, sc_conv_b, w_branch, w_out,
                       ln1_g, ln1_b, w_up, ffn_conv_w, ffn_conv_b, w_down, ln2_g, ln2_b, heads)
    mods = _ada(jnp.concatenate([c_prompt, c_sample], axis=0), w_ada, b_ada)
    cache_k = cache_fox_k.transpose(0, 1, 3, 4, 2).reshape(depth, bs, fw, past)
    cache_v = cache_fox_v.transpose(0, 1, 3, 4, 2).reshape(depth, bs, fw, past)
    lf_cache_t = jnp.swapaxes(cache_fox_logf, 2, 3)

    xp = x_prompt.reshape(bp * sp, d)
    xs = x_sample.reshape(bs * ss, d)
    zeros_sc = jnp.zeros((bp, 2, fw), F32)
    zeros_gla = jnp.zeros((bp,) + state_gla.shape[2:], F32)
    zeros_ffn = jnp.zeros((bp, 2, dff), F32)
    kp_all = jnp.zeros((depth, bp, fw, sp), F32)
    vp_all = jnp.zeros((depth, bp, fw, sp), F32)
    ks_all = jnp.zeros((depth, bs * ss, fw), F32)
    vs_all = jnp.zeros((depth, bs * ss, fw), F32)
    outs_p = [[] for _ in range(4)]
    outs_s = [[] for _ in range(4)]
    for l in range(depth):
        mod_p = mods[l, :bp].reshape(bp, 1, 6 * d)
        mod_s = mods[l, bp:].reshape(bs, 1, 6 * d)
        xp, kp_all, vp_all, st_p = _layer(l, xp, mod_p, wp, bp, sp, hd, alpha, None,
                                          zeros_sc, zeros_gla, zeros_ffn, kp_all, vp_all)
        xs, ks_all, vs_all, st_s = _layer(l, xs, mod_s, wp, bs, ss, hd, alpha,
                                          (cache_k, cache_v, cache_fox_logf, lf_cache_t),
                                          state_shortconv[l], state_gla, state_ffn_conv[l], ks_all, vs_all)
        for i in range(4):
            outs_p[i].append(st_p[i])
            outs_s[i].append(st_s[i])
    stacked_p = [jnp.stack(o) for o in outs_p]
    stacked_s = [jnp.stack(o) for o in outs_s]
    kv_p = lambda a: a.reshape(depth, bp, heads, hd, sp).transpose(0, 1, 4, 2, 3)
    kv_s = lambda a: a.reshape(depth, bs, ss, heads, hd)
    return (xp.reshape(bp, sp, d), xs.reshape(bs, ss, d),
            kv_p(kp_all), kv_p(vp_all), *stacked_p,
            kv_s(ks_all), kv_s(vs_all), *stacked_s)
```

```python
import functools

import jax
import jax.numpy as jnp
from jax import lax
from jax.experimental import pallas as pl
from jax.experimental.pallas import tpu as pltpu

F32 = jnp.float32
BF16 = jnp.bfloat16
LN_EPS = 1e-5
GLA_TAU = 16.0
LANES = 128
SUBLANES = 8
VMEM_LIMIT = 56 * 1024 * 1024
HIGHEST = lax.Precision.HIGHEST
LOG2E = 1.4426950408889634
FOX_UNDERFLOW = 140.0
FOX_NORM_MARGIN = 1.01

TM_PROJ = 512
TM_FFN = 512
TQ_FOX = 512
T_GLA = 512
GLA_CHUNK = 128
FFN_CHUNK = 256
ADA_TN = 1024

_NT = (((1,), (1,)), ((), ()))
_TN = (((0,), (0,)), ((), ()))


def _tile(n, pref):
    if n <= pref:
        return n
    t = pref
    while n % t:
        t //= 2
    return t


def _params(*sem):
    return pltpu.CompilerParams(dimension_semantics=sem, vmem_limit_bytes=VMEM_LIMIT)


def _resident(block, index_map):
    return pl.BlockSpec(block, index_map, pipeline_mode=pl.Buffered(1))


def _log_sigmoid(x):
    return jnp.minimum(x, 0.0) - jnp.log1p(jnp.exp(-jnp.abs(x)))


def _ln(x):
    mu = jnp.mean(x, axis=-1, keepdims=True)
    xc = x - mu
    var = jnp.mean(xc * xc, axis=-1, keepdims=True)
    return xc * lax.rsqrt(var + LN_EPS)


def _causal_conv3(u, p0, p1, w_ref, b_ref, cols):
    tm = u.shape[0]
    w0, w1, w2, bias = w_ref[0:1, cols], w_ref[1:2, cols], w_ref[2:3, cols], b_ref[:, cols]

    def taps(u0, u1, u2):
        y = bias + u2 * w0
        y = y + u1 * w1
        return y + u0 * w2

    head = u[0:SUBLANES, :]
    row = lax.broadcasted_iota(jnp.int32, (SUBLANES, 1), 0)
    h1 = jnp.where(row == 0, p1, pltpu.roll(head, 1, 0))
    h2 = jnp.where(row == 0, p0, jnp.where(row == 1, p1, pltpu.roll(head, 2, 0)))
    first = taps(head, h1, h2)
    if tm == SUBLANES:
        return first
    rest = taps(u, pltpu.roll(u, 1, 0), pltpu.roll(u, 2, 0))
    return jnp.concatenate([first, rest[SUBLANES:, :]], axis=0)


def _ada_kernel(c_ref, w_ref, b_ref, o_ref):
    c = c_ref[...]
    s = (c * jax.nn.sigmoid(c)).astype(BF16)
    o_ref[...] = jnp.dot(s, w_ref[...].astype(BF16), preferred_element_type=F32) + b_ref[...]


def _ada(c_all, w_ada, b_ada):
    depth, d, n6 = w_ada.shape
    r = c_all.shape[0]
    tn = _tile(n6, ADA_TN)
    return pl.pallas_call(
        _ada_kernel,
        grid=(depth, n6 // tn),
        in_specs=[pl.BlockSpec((r, d), lambda l, n: (0, 0)),
                  pl.BlockSpec((None, d, tn), lambda l, n: (l, 0, n)),
                  pl.BlockSpec((None, 1, tn), lambda l, n: (l, 0, n))],
        out_specs=pl.BlockSpec((None, r, tn), lambda l, n: (l, 0, n)),
        out_shape=jax.ShapeDtypeStruct((depth, r, n6), F32),
        compiler_params=_params("arbitrary", "arbitrary"),
        name="ada_mod",
    )(c_all, w_ada, b_ada.reshape(depth, 1, n6))


def _inproj_kernel(x_ref, mod_ref, w_ref, wff_ref, wglr_ref, bff_ref, wup_ref, ba_ref, scw_ref, scb_ref,
                   prev_ref, kf_all_ref, vf_all_ref,
                   q_ref, kf_ref, vf_ref, kb_ref, vb_ref, lf_ref, ysc_ref, scst_ref,
                   gq_ref, gk_ref, gv_ref, gg_ref, la_ref,
                   *tail, d, fw, hd, q_scale, gq_scale, transposed):
    del kf_all_ref, vf_all_ref
    carry_ref = tail[-1]
    j = pl.program_id(1)
    tm = x_ref.shape[0]
    h = _ln(x_ref[...]) * (1.0 + mod_ref[:, d:2 * d]) + mod_ref[:, 0:d]
    hb = h.astype(BF16)

    def proj(g):
        return jnp.dot(hb, w_ref[:, g * fw:(g + 1) * fw], preferred_element_type=F32)

    q = proj(0) * q_scale
    k = proj(1)
    v = proj(2)
    kb_ref[...] = k.astype(BF16)
    if transposed:
        vt = v.T
        q_ref[...] = q.T.astype(BF16)
        kf_ref[...] = k.T
        vf_ref[...] = vt
        vb_ref[...] = vt.astype(BF16)
        kk = k * k
        lane = lax.broadcasted_iota(jnp.int32, (1, LANES), 1)
        norms = jnp.zeros((1, LANES), F32)
        for hh in range(fw // hd):
            n2 = jnp.sum(kk[:, hh * hd:(hh + 1) * hd], axis=-1, keepdims=True)
            norms = jnp.where(lane == hh, jnp.max(n2, axis=0, keepdims=True), norms)
        tail[0][...] = norms
    else:
        q_ref[...] = q.astype(BF16)
        kf_ref[...] = k
        vf_ref[...] = v
        vb_ref[...] = v.astype(BF16)
    fft =lax.dot_general(wff_ref[...], hb, _NT, preferred_element_type=F32)
    lf_ref[...] = _log_sigmoid(fft + bff_ref[...])

    first = j == 0
    p0 = jnp.where(first, prev_ref[0:1, :], carry_ref[SUBLANES - 2:SUBLANES - 1, :])
    p1 = jnp.where(first, prev_ref[1:2, :], carry_ref[SUBLANES - 1:SUBLANES, :])
    sb = proj(3)
    u = proj(4) * proj(5)
    y = _causal_conv3(u, p0, p1, scw_ref, scb_ref, slice(None))
    ysc_ref[...] = (sb * y).astype(BF16)
    carry_ref[...] = u[tm - SUBLANES:tm, :]
    scst_ref[...] = carry_ref[SUBLANES - 2:SUBLANES, :]

    gq_ref[...] = (proj(6) * gq_scale).astype(BF16)
    gk_ref[...] = proj(7).astype(BF16)
    gv_ref[...] = proj(8).astype(BF16)
    gg = proj(9)
    gg_ref[...] = (gg * jax.nn.sigmoid(gg)).astype(BF16)
    glr = jnp.dot(hb, wglr_ref[...], preferred_element_type=F32)
    lap = jnp.dot(glr.astype(BF16), wup_ref[...], preferred_element_type=F32) + ba_ref[...]
    la_ref[...] = _log_sigmoid(lap) / GLA_TAU


def _inproj(l, x2d, mod, wp, sc_prev, kf_all, vf_all, nb, seq, hd, dk, transposed):
    n, d = x2d.shape
    fw = wp["fw"]
    heads = fw // hd
    tm = _tile(seq, TM_PROJ)
    nt = seq // tm
    rows = lambda b, j: (b * nt + j, 0)
    wl = lambda b, j: (l, 0, 0)
    act = lambda dt: jax.ShapeDtypeStruct((n, fw), dt)
    act_spec = pl.BlockSpec((tm, fw), rows)
    if transposed:
        t_shape, t_spec = jax.ShapeDtypeStruct((fw, n), BF16), pl.BlockSpec((fw, tm), lambda b, j: (0, b * nt + j))
        q_scale = LOG2E * hd ** -0.5
        stacked = pl.BlockSpec((None, None, fw, tm), lambda b, j: (l, b, 0, j))
    else:
        t_shape, t_spec, q_scale = act(BF16), act_spec, hd ** -0.5
        stacked = pl.BlockSpec((None, tm, fw), lambda b, j: (l, b * nt + j, 0))
    hbm = pl.BlockSpec(memory_space=pl.ANY)
    kern = functools.partial(_inproj_kernel, d=d, fw=fw, hd=hd, q_scale=q_scale, gq_scale=dk ** -0.5,
                             transposed=transposed)
    extra_specs = [pl.BlockSpec((None, 1, LANES), lambda b, j: (b * nt + j, 0, 0))] if transposed else []
    extra_shapes = [jax.ShapeDtypeStruct((nb * nt, 1, LANES), F32)] if transposed else []
    return pl.pallas_call(
        kern,
        grid=(nb, nt),
        in_specs=[pl.BlockSpec((tm, d), rows),
                  pl.BlockSpec((None, 1, 6 * d), lambda b, j: (b, 0, 0)),
                  _resident((None, d, 10 * fw), wl),
                  _resident((None, heads, d), wl),
                  _resident((None, d, LANES), wl),
                  _resident((None, heads, 1), wl),
                  _resident((None, LANES, fw), wl),
                  _resident((None, 1, fw), wl),
                  _resident((None, 3, fw), wl),
                  _resident((None, 1, fw), wl),
                  pl.BlockSpec((None, 2, fw), lambda b, j: (b, 0, 0)),
                  hbm, hbm],
        out_specs=[t_spec, stacked, stacked, act_spec, t_spec,
                   pl.BlockSpec((None, heads, tm), lambda b, j: (b * nt + j, 0, 0)),
                   act_spec,
                   pl.BlockSpec((None, 2, fw), lambda b, j: (b, 0, 0)),
                   act_spec, act_spec, act_spec, act_spec, act_spec] + extra_specs,
        out_shape=[t_shape, jax.ShapeDtypeStruct(kf_all.shape, F32), jax.ShapeDtypeStruct(vf_all.shape, F32),
                   act(BF16), t_shape,
                   jax.ShapeDtypeStruct((nb * nt, heads, tm), F32),
                   act(BF16),
                   jax.ShapeDtypeStruct((nb, 2, fw), F32),
                   act(BF16), act(BF16), act(BF16), act(BF16), act(F32)] + extra_shapes,
        input_output_aliases={11: 1, 12: 2},
        scratch_shapes=[pltpu.VMEM((SUBLANES, fw), F32)],
        compiler_params=_params("arbitrary", "arbitrary"),
        name="in_proj",
    )(x2d, mod, wp["w_main"], wp["w_fft"], wp["w_glr"], wp["b_fox"], wp["w_gla_up"], wp["b_gla"],
      wp["sc_w"], wp["sc_b"], sc_prev, kf_all, vf_all)


def _lane_cumsum(x, lane):
    for k in range(7):
        s = 1 << k
        x = x + jnp.where(lane >= s, pltpu.roll(x, s, 1), 0.0)
    return x


def _cumsum_kernel(x_ref, cp_ref, c1_ref, c2_ref, c3_ref):
    r, length = x_ref.shape
    lane = lax.broadcasted_iota(jnp.int32, (r, LANES), 1)

    def body(i, carry):
        off = pl.multiple_of(i * LANES, LANES)
        x = _lane_cumsum(x_ref[:, pl.ds(off, LANES)], lane) + carry
        c = x * LOG2E
        c1 = c.astype(BF16).astype(F32)
        r1 = c - c1
        c2 = r1.astype(BF16).astype(F32)
        c3 = (r1 - c2).astype(BF16).astype(F32)
        cp_ref[:, pl.ds(off, LANES)] = c
        c1_ref[:, pl.ds(off, LANES)] = c1
        c2_ref[:, pl.ds(off, LANES)] = c2
        c3_ref[:, pl.ds(off, LANES)] = c3
        return x[:, LANES - 1:LANES]

    lax.fori_loop(0, length // LANES, body, jnp.zeros((r, 1), F32), unroll=min(8, length // LANES))


def _cumsum_rows(x):
    r, length = x.shape
    spec = pl.BlockSpec((r, length), lambda i: (0, 0))
    return pl.pallas_call(
        _cumsum_kernel,
        grid=(1,),
        in_specs=[spec],
        out_specs=[spec, spec, spec, spec],
        out_shape=[jax.ShapeDtypeStruct((r, length), F32)] * 4,
        compiler_params=_params("arbitrary"),
        name="fox_cumsum",
    )(x)


def _fox_kernel(cend_ref, qt_ref, cq1_ref, cq2_ref, cq3_ref, k_ref, ck1_ref, ck2_ref, ck3_ref, vt_ref, o_ref,
                m_ref, acc_ref, qs_ref, ak_ref, *, hd):
    b = pl.program_id(0)
    pr = pl.program_id(1)
    i = pl.program_id(2)
    tq = qt_ref.shape[1]
    seq = k_ref.shape[0]
    nk = seq // tq
    lane = lax.broadcasted_iota(jnp.int32, (1, LANES), 1)
    head_a = lane < hd
    cqs = (cq1_ref, cq2_ref, cq3_ref)
    cks = (ck1_ref, ck2_ref, ck3_ref)

    group = 2 * SUBLANES

    @pl.when(i == 0)
    def _():
        rk = lax.broadcasted_iota(jnp.int32, (group, LANES), 0)
        gap = jnp.zeros((hd - group, LANES), F32)

        def chunk(t, carry):
            off = pl.multiple_of(t * LANES, LANES)
            parts = []
            for h in (1, 0):
                rows = jnp.where(rk < 3, 1.0, 0.0)
                for n, ck in enumerate(cks):
                    rows = jnp.where(rk == 3 + n, -ck[h, :, pl.ds(off, LANES)], rows)
                parts += [rows, gap]
            ak_ref[pl.ds(off, LANES), :] = jnp.concatenate(parts, axis=0).T.astype(BF16)
            return carry

        lax.fori_loop(0, seq // LANES, chunk, 0)

    rg = lax.broadcasted_iota(jnp.int32, (group, tq), 0)

    def bias_rows(h):
        rows = jnp.where(rg < 6, 1.0, 0.0)
        for n, cq in enumerate(cqs):
            rows = jnp.where(rg == n, cq[h], rows)
        return rows.astype(BF16)

    qs_ref[0, 0:hd, :] = qt_ref[0:hd, :]
    qs_ref[0, hd:hd + group, :] = bias_rows(0)
    qs_ref[0, hd + group:, :] = jnp.zeros((hd - group, tq), BF16)
    qs_ref[1, 0:group, :] = bias_rows(1)
    qs_ref[1, group:hd, :] = jnp.zeros((hd - group, tq), BF16)
    qs_ref[1, hd:, :] = qt_ref[hd:, :]
    m_ref[...] = jnp.full(m_ref.shape, -jnp.inf, F32)
    acc_ref[...] = jnp.zeros(acc_ref.shape, F32)

    def step(jk, tk, diagonal):
        off = pl.multiple_of(jk * tq, tq)
        ones = jnp.ones((hd, tk), BF16)
        k = k_ref[pl.ds(off, tk), :]
        ak = ak_ref[pl.ds(off, tk), :]
        ks = (jnp.where(head_a, k, ak), jnp.where(head_a, ak, k))
        vts = (jnp.concatenate([vt_ref[0:hd, pl.ds(off, tk)], ones], axis=0),
               jnp.concatenate([ones, vt_ref[hd:, pl.ds(off, tk)]], axis=0))
        sts = [jnp.dot(ks[h], qs_ref[h], preferred_element_type=F32) for h in range(2)]
        if diagonal:
            krow = lax.broadcasted_iota(jnp.int32, (tk, tq), 0)
            qcol = lax.broadcasted_iota(jnp.int32, (tk, tq), 1)
            sts = [jnp.where(krow <= qcol, st, -jnp.inf) for st in sts]
        m_prevs = [m_ref[h] for h in range(2)]
        m_news = [jnp.maximum(m_prevs[h], jnp.max(sts[h], axis=0, keepdims=True)) for h in range(2)]
        for h in range(2):
            p = jnp.exp2(sts[h] - m_news[h]).astype(BF16)
            alpha = jnp.exp2(m_prevs[h] - m_news[h])
            acc_ref[h] = alpha * acc_ref[h] + jnp.dot(vts[h], p, preferred_element_type=F32)
            m_ref[h] = m_news[h]

    step(i, tq, True)

    pair = b * pl.num_programs(1) + pr
    base = pair * 2 * nk
    kmax_base = pl.num_programs(0) * pl.num_programs(1) * 2 * nk + pair * 2
    slack = []
    for h in range(2):
        qf = qt_ref[h * hd:(h + 1) * hd, :].astype(F32)
        qn = jnp.sqrt(jnp.max(jnp.sum(qf * qf, axis=0, keepdims=True)))
        slack.append(qn * cend_ref[kmax_base + h] * FOX_NORM_MARGIN - jnp.min(m_ref[h]))
    before = jnp.maximum(i - 1, 0)

    def needed(j):
        r = False
        for h in range(2):
            gap = cend_ref[base + h * nk + before] - cend_ref[base + h * nk + j]
            r = jnp.logical_or(r, slack[h] + gap > -FOX_UNDERFLOW)
        return r

    j_lo = lax.while_loop(lambda j: jnp.logical_and(j > 0, needed(jnp.maximum(j - 1, 0))), lambda j: j - 1, i)
    n_off = i - j_lo

    def body(t, carry):
        step(j_lo + 2 * t, 2 * tq, False)
        return carry

    lax.fori_loop(0, n_off // 2, body, 0)

    @pl.when(n_off % 2 == 1)
    def _():
        step(i - 1, tq, False)

    acc_a = acc_ref[0]
    acc_b = acc_ref[1]
    out_t = jnp.concatenate([acc_a[0:hd] / acc_a[hd:hd + 1], acc_b[hd:] / acc_b[0:1]], axis=0)
    o_ref[...] = out_t.T.astype(BF16)


def _fox_prompt(cp, c1, c2, c3, knorm, qt, kb, vt, nb, seq, hd):
    fw, n = qt.shape
    pairs = fw // LANES
    tq = _tile(seq, TQ_FOX)
    nq = seq // tq
    by_pair = lambda a: a.reshape(pairs, 2, nb, 1, seq)
    cend = by_pair(cp)[:, :, :, 0, tq - 1::tq].transpose(2, 0, 1, 3).reshape(-1)
    cend = jnp.concatenate([cend, knorm.reshape(-1)])
    terms = [by_pair(c) for c in (c1, c2, c3)]
    qmap = lambda b, p, i, c: (p, b * nq + i)
    kmap = lambda b, p, i, c: (b, p)
    cq_spec = pl.BlockSpec((None, 2, None, 1, tq), lambda b, p, i, c: (p, 0, b, 0, i))
    ck_spec = pl.BlockSpec((None, 2, None, 1, seq), lambda b, p, i, c: (p, 0, b, 0, 0))
    grid_spec = pltpu.PrefetchScalarGridSpec(
        num_scalar_prefetch=1,
        grid=(nb, pairs, nq),
        in_specs=[pl.BlockSpec((LANES, tq), qmap), cq_spec, cq_spec, cq_spec,
                  pl.BlockSpec((seq, LANES), kmap), ck_spec, ck_spec, ck_spec,
                  pl.BlockSpec((LANES, seq), lambda b, p, i, c: (p, b))],
        out_specs=pl.BlockSpec((tq, LANES), lambda b, p, i, c: (b * nq + i, p)),
        scratch_shapes=[pltpu.VMEM((2, 1, tq), F32), pltpu.VMEM((2, LANES, tq), F32),
                        pltpu.VMEM((2, LANES, tq), BF16), pltpu.VMEM((seq, LANES), BF16)],
    )
    return pl.pallas_call(
        functools.partial(_fox_kernel, hd=hd),
        grid_spec=grid_spec,
        out_shape=jax.ShapeDtypeStruct((n, fw), BF16),
        compiler_params=_params("arbitrary", "arbitrary", "arbitrary"),
        name="fox_attn",
    )(cend, qt, *terms, kb, *terms, vt)


def _fox_cached_kernel(q_ref, kn_ref, vn_ref, kc_ref, vc_ref, lcn_ref, lct_ref, lnn_ref, lnt_ref, o_ref,
                       ck_ref, *, hd):
    ln, fw = q_ref.shape
    past = kc_ref.shape[1]
    heads = fw // hd
    lane8 = lax.broadcasted_iota(jnp.int32, (heads, LANES), 1)

    def body(i, carry):
        off = pl.multiple_of(i * LANES, LANES)
        x = _lane_cumsum(lct_ref[:, pl.ds(off, LANES)], lane8) + carry
        ck_ref[:, pl.ds(off, LANES)] = x
        return x[:, LANES - 1:LANES]

    c_last = lax.fori_loop(0, past // LANES, body, jnp.zeros((heads, 1), F32),
                           unroll=min(8, past // LANES))
    r_i = lax.broadcasted_iota(jnp.int32, (ln, ln), 0)
    c_i = lax.broadcasted_iota(jnp.int32, (ln, ln), 1)
    causal = c_i <= r_i
    tril = causal.astype(F32)
    triu = (r_i <= c_i).astype(F32)
    ck_new = c_last + jnp.dot(lnt_ref[...], triu, precision=HIGHEST, preferred_element_type=F32)
    cache_total = jnp.dot(jnp.ones((ln, past), F32), lcn_ref[...], precision=HIGHEST,
                          preferred_element_type=F32)
    cq = cache_total + jnp.dot(tril, lnn_ref[...], precision=HIGHEST, preferred_element_type=F32)

    outs = []
    for h in range(heads):
        feat = slice(h * hd, (h + 1) * hd)
        qh = q_ref[:, feat]
        kct = kc_ref[feat, :].astype(BF16)
        vct = vc_ref[feat, :].astype(BF16)
        cqh = cq[:, h:h + 1]
        s_c = jnp.dot(qh, kct, preferred_element_type=F32) + cqh - ck_ref[h:h + 1, :]
        s_n = lax.dot_general(qh, kn_ref[:, feat], _NT, preferred_element_type=F32) + cqh - ck_new[h:h + 1, :]
        s_n = jnp.where(causal, s_n, -jnp.inf)
        m = jnp.maximum(jnp.max(s_c, axis=-1, keepdims=True), jnp.max(s_n, axis=-1, keepdims=True))
        p_c = jnp.exp(s_c - m)
        p_n = jnp.exp(s_n - m)
        den = jnp.sum(p_c, axis=-1, keepdims=True) + jnp.sum(p_n, axis=-1, keepdims=True)
        acc = (lax.dot_general(p_c.astype(BF16), vct, _NT, preferred_element_type=F32)
               + jnp.dot(p_n.astype(BF16), vn_ref[:, feat], preferred_element_type=F32))
        outs.append(acc / den)
    o_ref[...] = jnp.concatenate(outs, axis=1).astype(BF16)


def _fox_cached(l, qb, kb, vb, cache_k, cache_v, lf_cache_nat, lf_cache_t, lf_new_nat, lf_new_t, nb, ln, hd):
    n, fw = qb.shape
    past = cache_k.shape[3]
    heads = fw // hd
    rows = lambda b: (b, 0)
    cache = lambda b: (l, b, 0, 0)
    act = pl.BlockSpec((ln, fw), rows)
    return pl.pallas_call(
        functools.partial(_fox_cached_kernel, hd=hd),
        grid=(nb,),
        in_specs=[act, act, act,
                  pl.BlockSpec((None, None, fw, past), cache),
                  pl.BlockSpec((None, None, fw, past), cache),
                  pl.BlockSpec((None, None, past, heads), cache),
                  pl.BlockSpec((None, None, heads, past), cache),
                  pl.BlockSpec((None, ln, heads), lambda b: (b, 0, 0)),
                  pl.BlockSpec((None, heads, ln), lambda b: (b, 0, 0))],
        out_specs=act,
        out_shape=jax.ShapeDtypeStruct((n, fw), BF16),
        scratch_shapes=[pltpu.VMEM((heads, past), F32)],
        compiler_params=_params("arbitrary"),
        name="fox_attn_cached",
    )(qb, kb, vb, cache_k, cache_v, lf_cache_nat, lf_cache_t, lf_new_nat, lf_new_t)


def _gla_kernel(q_ref, k_ref, v_ref, g_ref, la_ref, s0_ref, gn_ref, y_ref, sout_ref, st_ref,
                *, chunk, nh, dk, dv):
    j = pl.program_id(1)
    t_rows = q_ref.shape[0]

    @pl.when(j == 0)
    def _():
        for h in range(nh):
            st_ref[h] = s0_ref[h].T

    row = lax.broadcasted_iota(jnp.int32, (chunk, chunk), 0)
    col = lax.broadcasted_iota(jnp.int32, (chunk, chunk), 1)
    causal = col <= row
    tril = causal.astype(F32)
    mid = chunk // 2
    for c in range(t_rows // chunk):
        rows = slice(c * chunk, (c + 1) * chunk)
        for h in range(nh):
            kc = slice(h * dk, (h + 1) * dk)
            vc = slice(h * dv, (h + 1) * dv)
            b = jnp.dot(tril, la_ref[rows, kc], precision=HIGHEST, preferred_element_type=F32)
            b_last = b[chunk - 1:chunk, :]
            b_mid = b[mid:mid + 1, :]
            q = q_ref[rows, kc].astype(F32)
            k = k_ref[rows, kc].astype(F32)
            v = v_ref[rows, vc]
            qi = (q * jnp.exp(b - b_mid)).astype(BF16)
            ki = (k * jnp.exp(b_mid - b)).astype(BF16)
            a = lax.dot_general(qi, ki, _NT, preferred_element_type=F32)
            a = jnp.where(causal, a, 0.0)
            o = jnp.dot(a.astype(BF16), v, preferred_element_type=F32)
            st = st_ref[h]
            qe = (q * jnp.exp(b)).astype(BF16)
            o = o + lax.dot_general(qe, st.astype(BF16), _NT, preferred_element_type=F32)
            kd = (k * jnp.exp(b_last - b)).astype(BF16)
            st_ref[h] = st * jnp.exp(b_last) + lax.dot_general(v, kd, _TN, preferred_element_type=F32)
            ms = jnp.mean(o * o, axis=-1, keepdims=True)
            y = o * lax.rsqrt(ms + LN_EPS) * gn_ref[...]
            y_ref[rows, vc] = (y * g_ref[rows, vc].astype(F32)).astype(BF16)

    @pl.when(j == pl.num_programs(1) - 1)
    def _():
        for h in range(nh):
            sout_ref[h] = st_ref[h].T


def _gla(l, gq, gk, gv, gg, la, s0, gn, nb, seq, s0_layered):
    n, kw = gq.shape
    nh, dk, dv = s0.shape[-3:]
    t_rows = _tile(seq, T_GLA)
    chunk = _tile(t_rows, GLA_CHUNK)
    nt = seq // t_rows
    rows = lambda b, j: (b * nt + j, 0)
    act = pl.BlockSpec((t_rows, kw), rows)
    if s0_layered:
        s0_spec = pl.BlockSpec((None, None, nh, dk, dv), lambda b, j: (l, b, 0, 0, 0))
    else:
        s0_spec = pl.BlockSpec((None, nh, dk, dv), lambda b, j: (b, 0, 0, 0))
    return pl.pallas_call(
        functools.partial(_gla_kernel, chunk=chunk, nh=nh, dk=dk, dv=dv),
        grid=(nb, nt),
        in_specs=[act, act, act, act, act, s0_spec,
                  _resident((None, 1, dv), lambda b, j: (l, 0, 0))],
        out_specs=[act, pl.BlockSpec((None, nh, dk, dv), lambda b, j: (b, 0, 0, 0))],
        out_shape=[jax.ShapeDtypeStruct((n, kw), BF16), jax.ShapeDtypeStruct((nb, nh, dk, dv), F32)],
        scratch_shapes=[pltpu.VMEM((nh, dv, dk), F32)],
        compiler_params=_params("arbitrary", "arbitrary"),
        name="gla",
    )(gq, gk, gv, gg, la, s0, gn)


def _merge_ffn_kernel(x_ref, mod_ref, yf_ref, ys_ref, yg_ref, wm_ref, wb_ref, wo_ref, g1_ref, b1_ref,
                      wu_ref, cw_ref, cb_ref, wd_ref, g_ref, b_ref, prev_ref, o_ref, st_ref,
                      carry_ref, act_ref, *, d, dff, chunk, alpha):
    j = pl.program_id(1)
    tm = x_ref.shape[0]
    first = j == 0
    x0 = x_ref[...]
    hb = (_ln(x0) * (1.0 + mod_ref[:, d:2 * d]) + mod_ref[:, 0:d]).astype(BF16)
    merged = None
    for i, y_ref in enumerate((yf_ref, ys_ref, yg_ref)):
        gate = jax.nn.sigmoid(jnp.dot(hb, wm_ref[:, i * d:(i + 1) * d], preferred_element_type=F32))
        term = gate * jnp.dot(y_ref[...], wb_ref[i], preferred_element_type=F32)
        merged = term if merged is None else merged + term
    mix = jnp.dot(merged.astype(BF16), wo_ref[...], preferred_element_type=F32)
    x = _ln(alpha * x0 + mod_ref[:, 2 * d:3 * d] * mix) * g1_ref[...] + b1_ref[...]

    hb = (_ln(x) * (1.0 + mod_ref[:, 4 * d:5 * d]) + mod_ref[:, 3 * d:4 * d]).astype(BF16)
    for c0 in range(0, dff, chunk):
        cols = slice(c0, c0 + chunk)
        ug = jnp.dot(hb, wu_ref[:, cols], preferred_element_type=F32)
        uv = jnp.dot(hb, wu_ref[:, dff + c0:dff + c0 + chunk], preferred_element_type=F32)
        p0 = jnp.where(first, prev_ref[0:1, cols], carry_ref[SUBLANES - 2:SUBLANES - 1, cols])
        p1 = jnp.where(first, prev_ref[1:2, cols], carry_ref[SUBLANES - 1:SUBLANES, cols])
        ugc = _causal_conv3(ug, p0, p1, cw_ref, cb_ref, cols)
        carry_ref[:, cols] = ug[tm - SUBLANES:tm, :]
        act_ref[:, cols] = (jax.nn.gelu(ugc) * uv).astype(BF16)
    f = jnp.dot(act_ref[...], wd_ref[...], preferred_element_type=F32)
    st_ref[...] = carry_ref[SUBLANES - 2:SUBLANES, :]
    z = alpha * x + mod_ref[:, 5 * d:6 * d] * f
    o_ref[...] = _ln(z) * g_ref[...] + b_ref[...]


def _merge_ffn(l, x2d, mod, y_fox, y_sc, y_gla, wp, prev, nb, seq, alpha):
    n, d = x2d.shape
    bw = y_fox.shape[1]
    dff = wp["w_down"].shape[1]
    tm = _tile(seq, TM_FFN)
    nt = seq // tm
    chunk = _tile(dff, FFN_CHUNK)
    rows = lambda b, j: (b * nt + j, 0)
    wl3 = lambda b, j: (l, 0, 0)
    yspec = pl.BlockSpec((tm, bw), rows)
    st = pl.BlockSpec((None, 2, dff), lambda b, j: (b, 0, 0))
    return pl.pallas_call(
        functools.partial(_merge_ffn_kernel, d=d, dff=dff, chunk=chunk, alpha=alpha),
        grid=(nb, nt),
        in_specs=[pl.BlockSpec((tm, d), rows),
                  pl.BlockSpec((None, 1, 6 * d), lambda b, j: (b, 0, 0)),
                  yspec, yspec, yspec,
                  _resident((None, d, 3 * d), wl3),
                  _resident((None, 3, bw, d), lambda b, j: (l, 0, 0, 0)),
                  _resident((None, d, d), wl3),
                  _resident((None, 1, d), wl3),
                  _resident((None, 1, d), wl3),
                  _resident((None, d, 2 * dff), wl3),
                  _resident((None, 3, dff), wl3),
                  _resident((None, 1, dff), wl3),
                  _resident((None, dff, d), wl3),
                  _resident((None, 1, d), wl3),
                  _resident((None, 1, d), wl3),
                  st],
        out_specs=[pl.BlockSpec((tm, d), rows), st],
        out_shape=[jax.ShapeDtypeStruct((n, d), F32), jax.ShapeDtypeStruct((nb, 2, dff), F32)],
        scratch_shapes=[pltpu.VMEM((SUBLANES, dff), F32), pltpu.VMEM((tm, dff), BF16)],
        compiler_params=_params("arbitrary", "arbitrary"),
        name="merge_ffn",
    )(x2d, mod, y_fox, y_sc, y_gla, wp["w_merge"], wp["w_branch"], wp["w_out"], wp["ln1_g"], wp["ln1_b"],
      wp["w_up"], wp["ffn_conv_w"], wp["ffn_conv_b"], wp["w_down"], wp["ln2_g"], wp["ln2_b"], prev)


def _pack_weights(w_in, b_fox_f, w_gla_up, b_gla_a, gla_norm_g, sc_conv_w, sc_conv_b, w_branch, w_out,
                  ln1_g, ln1_b, w_up, ffn_conv_w, ffn_conv_b, w_down, ln2_g, ln2_b, heads):
    depth, d, _ = w_in.shape
    fw = sc_conv_w.shape[-1]
    rank = w_gla_up.shape[1]
    o = 0
    fox = w_in[:, :, o:o + 3 * fw]; o += 3 * fw
    wff = w_in[:, :, o:o + heads]; o += heads
    sc = w_in[:, :, o:o + 3 * fw]; o += 3 * fw
    gla = w_in[:, :, o:o + 4 * fw]; o += 4 * fw
    glr = w_in[:, :, o:o + rank]; o += rank
    wm = w_in[:, :, o:]
    row = lambda a: a.reshape(depth, 1, a.shape[-1])
    return {
        "fw": fw,
        "w_main": jnp.concatenate([fox, sc, gla], axis=-1).astype(BF16),
        "w_fft": jnp.swapaxes(wff, 1, 2).astype(BF16),
        "w_glr": jnp.pad(glr, ((0, 0), (0, 0), (0, LANES - rank))).astype(BF16),
        "b_fox": b_fox_f.reshape(depth, heads, 1),
        "w_gla_up": jnp.pad(w_gla_up, ((0, 0), (0, LANES - rank), (0, 0))).astype(BF16),
        "b_gla": row(b_gla_a),
        "gla_norm_g": row(gla_norm_g),
        "sc_w": sc_conv_w, "sc_b": row(sc_conv_b),
        "w_merge": wm.astype(BF16),
        "w_branch": w_branch.astype(BF16),
        "w_out": w_out.astype(BF16),
        "ln1_g": row(ln1_g), "ln1_b": row(ln1_b),
        "w_up": w_up.astype(BF16),
        "ffn_conv_w": ffn_conv_w, "ffn_conv_b": row(ffn_conv_b),
        "w_down": w_down.astype(BF16),
        "ln2_g": row(ln2_g), "ln2_b": row(ln2_b),
    }


def _layer(l, x2d, mod, wp, nb, seq, hd, alpha, fox_cache, sc_prev, gla_s0, ffn_prev, kf_all, vf_all):
    fw = wp["fw"]
    heads = fw // hd
    dk = gla_s0.shape[-2]
    prompt = fox_cache is None
    (q, kf_all, vf_all, kb, v, lft, y_sc, sc_new, gq, gk, gv, gg, la, *kn2) = _inproj(
        l, x2d, mod, wp, sc_prev, kf_all, vf_all, nb, seq, hd, dk, prompt)
    lf_t = lft.reshape(nb, -1, heads, lft.shape[-1]).transpose(2, 0, 1, 3).reshape(heads, nb, seq)
    logf = lf_t.transpose(1, 2, 0)
    if prompt:
        cp, c1, c2, c3 = _cumsum_rows(lf_t.reshape(heads * nb, seq))
        knorm = jnp.sqrt(jnp.max(kn2[0][:, 0, :heads].reshape(nb, -1, heads), axis=1))
        y_fox = _fox_prompt(cp, c1, c2, c3, knorm, q, kb, v, nb, seq, hd)
        y_gla, gla_new = _gla(l, gq, gk, gv, gg, la, gla_s0, wp["gla_norm_g"], nb, seq, False)
    else:
        cache_k, cache_v, lf_cache_nat, lf_cache_t = fox_cache
        y_fox = _fox_cached(l, q, kb, v, cache_k, cache_v, lf_cache_nat, lf_cache_t,
                            logf, lf_t.transpose(1, 0, 2), nb, seq, hd)
        y_gla, gla_new = _gla(l, gq, gk, gv, gg, la, gla_s0, wp["gla_norm_g"], nb, seq, True)
    x2, ffn_new = _merge_ffn(l, x2d, mod, y_fox, y_sc, y_gla, wp, ffn_prev, nb, seq, alpha)
    return x2, kf_all, vf_all, (logf, sc_new, gla_new, ffn_new)


def kernel(x_prompt, x_sample, c_prompt, c_sample, cache_fox_k, cache_fox_v, cache_fox_logf, state_shortconv, state_gla, state_ffn_conv, w_ada, b_ada, w_in, b_fox_f, w_gla_up, b_gla_a, gla_norm_g, sc_conv_w, sc_conv_b, w_branch, w_out, ln1_g, ln1_b, w_up, ffn_conv_w, ffn_conv_b, w_down, ln2_g, ln2_b):
    bp, sp, d = x_prompt.shape
    bs, ss, _ = x_sample.shape
    depth = w_in.shape[0]
    _, _, past, heads, hd = cache_fox_k.shape
    fw = heads * hd
    dff = ffn_conv_w.shape[-1]
    assert sc_conv_w.shape[-1] == fw and state_gla.shape[2] * state_gla.shape[3] == fw
    assert state_gla.shape[2] * state_gla.shape[4] == fw and fw % LANES == 0 and 2 * hd == LANES
    assert sc_conv_w.shape[1] == 3 and ffn_conv_w.shape[1] == 3
    alpha = (2.0 * depth) ** 0.25

    wp = _pack_weights(w_in, b_fox_f, w_gla_up, b_gla_a, gla_norm_g, sc_conv_w, sc_conv_b, w_branch, w_out,
                       ln1_g, ln1_b, w_up, ffn_conv_w, ffn_conv_b, w_down, ln2_g, ln2_b, heads)
    mods = _ada(jnp.concatenate([c_prompt, c_sample], axis=0), w_ada, b_ada)
    cache_k = cache_fox_k.transpose(0, 1, 3, 4, 2).reshape(depth, bs, fw, past)
    cache_v = cache_fox_v.transpose(0, 1, 3, 4, 2).reshape(depth, bs, fw, past)
    lf_cache_t = jnp.swapaxes(cache_fox_logf, 2, 3)

    xp = x_prompt.reshape(bp * sp, d)
    xs = x_sample.reshape(bs * ss, d)
    zeros_sc = jnp.zeros((bp, 2, fw), F32)
    zeros_gla = jnp.zeros((bp,) + state_gla.shape[2:], F32)
    zeros_ffn = jnp.zeros((bp, 2, dff), F32)
    kp_all = jnp.zeros((depth, bp, fw, sp), F32)
    vp_all = jnp.zeros((depth, bp, fw, sp), F32)
    ks_all = jnp.zeros((depth, bs * ss, fw), F32)
    vs_all = jnp.zeros((depth, bs * ss, fw), F32)
    outs_p = [[] for _ in range(4)]
    outs_s = [[] for _ in range(4)]
    for l in range(depth):
        mod_p = mods[l, :bp].reshape(bp, 1, 6 * d)
        mod_s = mods[l, bp:].reshape(bs, 1, 6 * d)
        xp, kp_all, vp_all, st_p = _layer(l, xp, mod_p, wp, bp, sp, hd, alpha, None,
                                          zeros_sc, zeros_gla, zeros_ffn, kp_all, vp_all)
        xs, ks_all, vs_all, st_s = _layer(l, xs, mod_s, wp, bs, ss, hd, alpha,
                                          (cache_k, cache_v, cache_fox_logf, lf_cache_t),
                                          state_shortconv[l], state_gla, state_ffn_conv[l], ks_all, vs_all)
        for i in range(4):
            outs_p[i].append(st_p[i])
            outs_s[i].append(st_s[i])
    stacked_p = [jnp.stack(o) for o in outs_p]
    stacked_s = [jnp.stack(o) for o in outs_s]
    kv_p = lambda a: a.reshape(depth, bp, heads, hd, sp).transpose(0, 1, 4, 2, 3)
    kv_s = lambda a: a.reshape(depth, bs, ss, heads, hd)
    return (xp.reshape(bp, sp, d), xs.reshape(bs, ss, d),
            kv_p(kp_all), kv_p(vp_all), *stacked_p,
            kv_s(ks_all), kv_s(vs_all), *stacked_s)
```

```python
import functools

import jax
import jax.numpy as jnp
from jax import lax
from jax.experimental import pallas as pl
from jax.experimental.pallas import tpu as pltpu

F32 = jnp.float32
BF16 = jnp.bfloat16
LN_EPS = 1e-5
GLA_TAU = 16.0
LANES = 128
SUBLANES = 8
VMEM_LIMIT = 56 * 1024 * 1024
HIGHEST = lax.Precision.HIGHEST
LOG2E = 1.4426950408889634
FOX_UNDERFLOW = 140.0
FOX_NORM_MARGIN = 1.01

TM_PROJ = 512
TM_FFN = 512
TQ_FOX = 512
T_GLA = 1024
GLA_CHUNK = 128
FFN_CHUNK = 256
ADA_TN = 1024

_NT = (((1,), (1,)), ((), ()))
_TN = (((0,), (0,)), ((), ()))


def _tile(n, pref):
    if n <= pref:
        return n
    t = pref
    while n % t:
        t //= 2
    return t


def _params(*sem):
    return pltpu.CompilerParams(dimension_semantics=sem, vmem_limit_bytes=VMEM_LIMIT)


def _resident(block, index_map):
    return pl.BlockSpec(block, index_map, pipeline_mode=pl.Buffered(1))


def _log_sigmoid(x):
    return jnp.minimum(x, 0.0) - jnp.log1p(jnp.exp(-jnp.abs(x)))


def _ln(x):
    mu = jnp.mean(x, axis=-1, keepdims=True)
    xc = x - mu
    var = jnp.mean(xc * xc, axis=-1, keepdims=True)
    return xc * lax.rsqrt(var + LN_EPS)


def _causal_conv3(u, p0, p1, w_ref, b_ref, cols):
    tm = u.shape[0]
    w0, w1, w2, bias = w_ref[0:1, cols], w_ref[1:2, cols], w_ref[2:3, cols], b_ref[:, cols]

    def taps(u0, u1, u2):
        y = bias + u2 * w0
        y = y + u1 * w1
        return y + u0 * w2

    head = u[0:SUBLANES, :]
    row = lax.broadcasted_iota(jnp.int32, (SUBLANES, 1), 0)
    h1 = jnp.where(row == 0, p1, pltpu.roll(head, 1, 0))
    h2 = jnp.where(row == 0, p0, jnp.where(row == 1, p1, pltpu.roll(head, 2, 0)))
    first = taps(head, h1, h2)
    if tm == SUBLANES:
        return first
    rest = taps(u, pltpu.roll(u, 1, 0), pltpu.roll(u, 2, 0))
    return jnp.concatenate([first, rest[SUBLANES:, :]], axis=0)


def _ada_kernel(c_ref, w_ref, b_ref, o_ref):
    c = c_ref[...]
    s = (c * jax.nn.sigmoid(c)).astype(BF16)
    o_ref[...] = jnp.dot(s, w_ref[...].astype(BF16), preferred_element_type=F32) + b_ref[...]


def _ada(c_all, w_ada, b_ada):
    depth, d, n6 = w_ada.shape
    r = c_all.shape[0]
    tn = _tile(n6, ADA_TN)
    return pl.pallas_call(
        _ada_kernel,
        grid=(depth, n6 // tn),
        in_specs=[pl.BlockSpec((r, d), lambda l, n: (0, 0)),
                  pl.BlockSpec((None, d, tn), lambda l, n: (l, 0, n)),
                  pl.BlockSpec((None, 1, tn), lambda l, n: (l, 0, n))],
        out_specs=pl.BlockSpec((None, r, tn), lambda l, n: (l, 0, n)),
        out_shape=jax.ShapeDtypeStruct((depth, r, n6), F32),
        compiler_params=_params("arbitrary", "arbitrary"),
        name="ada_mod",
    )(c_all, w_ada, b_ada.reshape(depth, 1, n6))


def _inproj_kernel(x_ref, mod_ref, w_ref, wff_ref, wglr_ref, bff_ref, wup_ref, ba_ref, scw_ref, scb_ref,
                   prev_ref, kf_all_ref, vf_all_ref,
                   q_ref, kf_ref, vf_ref, kb_ref, vb_ref, lf_ref, ysc_ref, scst_ref,
                   gq_ref, gk_ref, gv_ref, gg_ref, la_ref,
                   *tail, d, fw, hd, q_scale, gq_scale, transposed):
    del kf_all_ref, vf_all_ref
    carry_ref = tail[-1]
    j = pl.program_id(1)
    tm = x_ref.shape[0]
    h = _ln(x_ref[...]) * (1.0 + mod_ref[:, d:2 * d]) + mod_ref[:, 0:d]
    hb = h.astype(BF16)

    def proj(g):
        return jnp.dot(hb, w_ref[:, g * fw:(g + 1) * fw], preferred_element_type=F32)

    q = proj(0) * q_scale
    k = proj(1)
    v = proj(2)
    kb_ref[...] = k.astype(BF16)
    if transposed:
        vt = v.T
        q_ref[...] = q.T.astype(BF16)
        kf_ref[...] = k.T
        vf_ref[...] = vt
        vb_ref[...] = vt.astype(BF16)
        kk = k * k
        lane = lax.broadcasted_iota(jnp.int32, (1, LANES), 1)
        norms = jnp.zeros((1, LANES), F32)
        for hh in range(fw // hd):
            n2 = jnp.sum(kk[:, hh * hd:(hh + 1) * hd], axis=-1, keepdims=True)
            norms = jnp.where(lane == hh, jnp.max(n2, axis=0, keepdims=True), norms)
        tail[0][...] = norms
    else:
        q_ref[...] = q.astype(BF16)
        kf_ref[...] = k
        vf_ref[...] = v
        vb_ref[...] = v.astype(BF16)
    fft =lax.dot_general(wff_ref[...], hb, _NT, preferred_element_type=F32)
    lf_ref[...] = _log_sigmoid(fft + bff_ref[...])

    first = j == 0
    p0 = jnp.where(first, prev_ref[0:1, :], carry_ref[SUBLANES - 2:SUBLANES - 1, :])
    p1 = jnp.where(first, prev_ref[1:2, :], carry_ref[SUBLANES - 1:SUBLANES, :])
    sb = proj(3)
    u = proj(4) * proj(5)
    y = _causal_conv3(u, p0, p1, scw_ref, scb_ref, slice(None))
    ysc_ref[...] = (sb * y).astype(BF16)
    carry_ref[...] = u[tm - SUBLANES:tm, :]
    scst_ref[...] = carry_ref[SUBLANES - 2:SUBLANES, :]

    gq_ref[...] = (proj(6) * gq_scale).astype(BF16)
    gk_ref[...] = proj(7).astype(BF16)
    gv_ref[...] = proj(8).astype(BF16)
    gg = proj(9)
    gg_ref[...] = (gg * jax.nn.sigmoid(gg)).astype(BF16)
    glr = jnp.dot(hb, wglr_ref[...], preferred_element_type=F32)
    lap = jnp.dot(glr.astype(BF16), wup_ref[...], preferred_element_type=F32) + ba_ref[...]
    la_ref[...] = _log_sigmoid(lap) / GLA_TAU


def _inproj(l, x2d, mod, wp, sc_prev, kf_all, vf_all, nb, seq, hd, dk, transposed):
    n, d = x2d.shape
    fw = wp["fw"]
    heads = fw // hd
    tm = _tile(seq, TM_PROJ)
    nt = seq // tm
    rows = lambda b, j: (b * nt + j, 0)
    wl = lambda b, j: (l, 0, 0)
    act = lambda dt: jax.ShapeDtypeStruct((n, fw), dt)
    act_spec = pl.BlockSpec((tm, fw), rows)
    if transposed:
        t_shape, t_spec = jax.ShapeDtypeStruct((fw, n), BF16), pl.BlockSpec((fw, tm), lambda b, j: (0, b * nt + j))
        q_scale = LOG2E * hd ** -0.5
        stacked = pl.BlockSpec((None, None, fw, tm), lambda b, j: (l, b, 0, j))
    else:
        t_shape, t_spec, q_scale = act(BF16), act_spec, hd ** -0.5
        stacked = pl.BlockSpec((None, tm, fw), lambda b, j: (l, b * nt + j, 0))
    hbm = pl.BlockSpec(memory_space=pl.ANY)
    kern = functools.partial(_inproj_kernel, d=d, fw=fw, hd=hd, q_scale=q_scale, gq_scale=dk ** -0.5,
                             transposed=transposed)
    extra_specs = [pl.BlockSpec((None, 1, LANES), lambda b, j: (b * nt + j, 0, 0))] if transposed else []
    extra_shapes = [jax.ShapeDtypeStruct((nb * nt, 1, LANES), F32)] if transposed else []
    return pl.pallas_call(
        kern,
        grid=(nb, nt),
        in_specs=[pl.BlockSpec((tm, d), rows),
                  pl.BlockSpec((None, 1, 6 * d), lambda b, j: (b, 0, 0)),
                  _resident((None, d, 10 * fw), wl),
                  _resident((None, heads, d), wl),
                  _resident((None, d, LANES), wl),
                  _resident((None, heads, 1), wl),
                  _resident((None, LANES, fw), wl),
                  _resident((None, 1, fw), wl),
                  _resident((None, 3, fw), wl),
                  _resident((None, 1, fw), wl),
                  pl.BlockSpec((None, 2, fw), lambda b, j: (b, 0, 0)),
                  hbm, hbm],
        out_specs=[t_spec, stacked, stacked, act_spec, t_spec,
                   pl.BlockSpec((None, heads, tm), lambda b, j: (b * nt + j, 0, 0)),
                   act_spec,
                   pl.BlockSpec((None, 2, fw), lambda b, j: (b, 0, 0)),
                   act_spec, act_spec, act_spec, act_spec, act_spec] + extra_specs,
        out_shape=[t_shape, jax.ShapeDtypeStruct(kf_all.shape, F32), jax.ShapeDtypeStruct(vf_all.shape, F32),
                   act(BF16), t_shape,
                   jax.ShapeDtypeStruct((nb * nt, heads, tm), F32),
                   act(BF16),
                   jax.ShapeDtypeStruct((nb, 2, fw), F32),
                   act(BF16), act(BF16), act(BF16), act(BF16), act(F32)] + extra_shapes,
        input_output_aliases={11: 1, 12: 2},
        scratch_shapes=[pltpu.VMEM((SUBLANES, fw), F32)],
        compiler_params=_params("arbitrary", "arbitrary"),
        name="in_proj",
    )(x2d, mod, wp["w_main"], wp["w_fft"], wp["w_glr"], wp["b_fox"], wp["w_gla_up"], wp["b_gla"],
      wp["sc_w"], wp["sc_b"], sc_prev, kf_all, vf_all)


def _lane_cumsum(x, lane):
    for k in range(7):
        s = 1 << k
        x = x + jnp.where(lane >= s, pltpu.roll(x, s, 1), 0.0)
    return x


def _cumsum_kernel(x_ref, cp_ref, c1_ref, c2_ref, c3_ref):
    r, length = x_ref.shape
    lane = lax.broadcasted_iota(jnp.int32, (r, LANES), 1)

    def body(i, carry):
        off = pl.multiple_of(i * LANES, LANES)
        x = _lane_cumsum(x_ref[:, pl.ds(off, LANES)], lane) + carry
        c = x * LOG2E
        c1 = c.astype(BF16).astype(F32)
        r1 = c - c1
        c2 = r1.astype(BF16).astype(F32)
        c3 = (r1 - c2).astype(BF16).astype(F32)
        cp_ref[:, pl.ds(off, LANES)] = c
        c1_ref[:, pl.ds(off, LANES)] = c1
        c2_ref[:, pl.ds(off, LANES)] = c2
        c3_ref[:, pl.ds(off, LANES)] = c3
        return x[:, LANES - 1:LANES]

    lax.fori_loop(0, length // LANES, body, jnp.zeros((r, 1), F32), unroll=min(8, length // LANES))


def _cumsum_rows(x):
    r, length = x.shape
    spec = pl.BlockSpec((r, length), lambda i: (0, 0))
    return pl.pallas_call(
        _cumsum_kernel,
        grid=(1,),
        in_specs=[spec],
        out_specs=[spec, spec, spec, spec],
        out_shape=[jax.ShapeDtypeStruct((r, length), F32)] * 4,
        compiler_params=_params("arbitrary"),
        name="fox_cumsum",
    )(x)


def _fox_kernel(cend_ref, qt_ref, cq1_ref, cq2_ref, cq3_ref, k_ref, ck1_ref, ck2_ref, ck3_ref, vt_ref, o_ref,
                m_ref, acc_ref, qs_ref, ak_ref, *, hd):
    b = pl.program_id(0)
    pr = pl.program_id(1)
    i = pl.program_id(2)
    tq = qt_ref.shape[1]
    seq = k_ref.shape[0]
    sub = tq // 2
    nk = seq // sub
    lane = lax.broadcasted_iota(jnp.int32, (1, LANES), 1)
    head_a = lane < hd
    cqs = (cq1_ref, cq2_ref, cq3_ref)
    cks = (ck1_ref, ck2_ref, ck3_ref)

    group = 2 * SUBLANES

    @pl.when(i == 0)
    def _():
        rk = lax.broadcasted_iota(jnp.int32, (group, LANES), 0)
        gap = jnp.zeros((hd - group, LANES), F32)

        def chunk(t, carry):
            off = pl.multiple_of(t * LANES, LANES)
            parts = []
            for h in (1, 0):
                rows = jnp.where(rk < 3, 1.0, 0.0)
                for n, ck in enumerate(cks):
                    rows = jnp.where(rk == 3 + n, -ck[h, :, pl.ds(off, LANES)], rows)
                parts += [rows, gap]
            ak_ref[pl.ds(off, LANES), :] = jnp.concatenate(parts, axis=0).T.astype(BF16)
            return carry

        lax.fori_loop(0, seq // LANES, chunk, 0)

    rg = lax.broadcasted_iota(jnp.int32, (group, tq), 0)

    def bias_rows(h):
        rows = jnp.where(rg < 6, 1.0, 0.0)
        for n, cq in enumerate(cqs):
            rows = jnp.where(rg == n, cq[h], rows)
        return rows.astype(BF16)

    qs_ref[0, 0:hd, :] = qt_ref[0:hd, :]
    qs_ref[0, hd:hd + group, :] = bias_rows(0)
    qs_ref[0, hd + group:, :] = jnp.zeros((hd - group, tq), BF16)
    qs_ref[1, 0:group, :] = bias_rows(1)
    qs_ref[1, group:hd, :] = jnp.zeros((hd - group, tq), BF16)
    qs_ref[1, hd:, :] = qt_ref[hd:, :]
    m_ref[...] = jnp.full(m_ref.shape, -jnp.inf, F32)
    acc_ref[...] = jnp.zeros(acc_ref.shape, F32)

    def step(jk, tk, diagonal):
        off = pl.multiple_of(jk * sub, sub)
        ones = jnp.ones((hd, tk), BF16)
        k = k_ref[pl.ds(off, tk), :]
        ak = ak_ref[pl.ds(off, tk), :]
        ks = (jnp.where(head_a, k, ak), jnp.where(head_a, ak, k))
        vts = (jnp.concatenate([vt_ref[0:hd, pl.ds(off, tk)], ones], axis=0),
               jnp.concatenate([ones, vt_ref[hd:, pl.ds(off, tk)]], axis=0))
        sts = [jnp.dot(ks[h], qs_ref[h], preferred_element_type=F32) for h in range(2)]
        if diagonal:
            krow = lax.broadcasted_iota(jnp.int32, (tk, tq), 0)
            qcol = lax.broadcasted_iota(jnp.int32, (tk, tq), 1)
            sts = [jnp.where(krow <= qcol, st, -jnp.inf) for st in sts]
        m_prevs = [m_ref[h] for h in range(2)]
        m_news = [jnp.maximum(m_prevs[h], jnp.max(sts[h], axis=0, keepdims=True)) for h in range(2)]
        for h in range(2):
            p = jnp.exp2(sts[h] - m_news[h]).astype(BF16)
            alpha = jnp.exp2(m_prevs[h] - m_news[h])
            acc_ref[h] = alpha * acc_ref[h] + jnp.dot(vts[h], p, preferred_element_type=F32)
            m_ref[h] = m_news[h]

    first = 2 * i
    step(first, tq, True)

    pair = b * pl.num_programs(1) + pr
    base = pair * 2 * nk
    kmax_base = pl.num_programs(0) * pl.num_programs(1) * 2 * nk + pair * 2
    slack = []
    for h in range(2):
        qf = qt_ref[h * hd:(h + 1) * hd, :].astype(F32)
        qn = jnp.sqrt(jnp.max(jnp.sum(qf * qf, axis=0, keepdims=True)))
        slack.append(qn * cend_ref[kmax_base + h] * FOX_NORM_MARGIN - jnp.min(m_ref[h]))
    before = jnp.maximum(first - 1, 0)

    def needed(j):
        r = False
        for h in range(2):
            gap = cend_ref[base + h * nk + before] - cend_ref[base + h * nk + j]
            r = jnp.logical_or(r, slack[h] + gap > -FOX_UNDERFLOW)
        return r

    j_lo = lax.while_loop(lambda j: jnp.logical_and(j > 0, needed(jnp.maximum(j - 1, 0))), lambda j: j - 1,
                          first)
    n_off = first - j_lo
    rem = n_off % 4
    for r in (1, 2, 3):
        @pl.when(rem == r)
        def _(r=r):
            step(j_lo, r * sub, False)

    def body(t, carry):
        step(j_lo + rem + 4 * t, 4 * sub, False)
        return carry

    lax.fori_loop(0, n_off // 4, body, 0)

    acc_a = acc_ref[0]
    acc_b = acc_ref[1]
    out_t = jnp.concatenate([acc_a[0:hd] / acc_a[hd:hd + 1], acc_b[hd:] / acc_b[0:1]], axis=0)
    o_ref[...] = out_t.T.astype(BF16)


def _fox_prompt(cp, c1, c2, c3, knorm, qt, kb, vt, nb, seq, hd):
    fw, n = qt.shape
    pairs = fw // LANES
    tq = _tile(seq, TQ_FOX)
    nq = seq // tq
    by_pair = lambda a: a.reshape(pairs, 2, nb, 1, seq)
    sub = tq // 2
    cend = by_pair(cp)[:, :, :, 0, sub - 1::sub].transpose(2, 0, 1, 3).reshape(-1)
    cend = jnp.concatenate([cend, knorm.reshape(-1)])
    terms = [by_pair(c) for c in (c1, c2, c3)]
    qmap = lambda b, p, i, c: (p, b * nq + i)
    kmap = lambda b, p, i, c: (b, p)
    cq_spec = pl.BlockSpec((None, 2, None, 1, tq), lambda b, p, i, c: (p, 0, b, 0, i))
    ck_spec = pl.BlockSpec((None, 2, None, 1, seq), lambda b, p, i, c: (p, 0, b, 0, 0))
    grid_spec = pltpu.PrefetchScalarGridSpec(
        num_scalar_prefetch=1,
        grid=(nb, pairs, nq),
        in_specs=[pl.BlockSpec((LANES, tq), qmap), cq_spec, cq_spec, cq_spec,
                  pl.BlockSpec((seq, LANES), kmap), ck_spec, ck_spec, ck_spec,
                  pl.BlockSpec((LANES, seq), lambda b, p, i, c: (p, b))],
        out_specs=pl.BlockSpec((tq, LANES), lambda b, p, i, c: (b * nq + i, p)),
        scratch_shapes=[pltpu.VMEM((2, 1, tq), F32), pltpu.VMEM((2, LANES, tq), F32),
                        pltpu.VMEM((2, LANES, tq), BF16), pltpu.VMEM((seq, LANES), BF16)],
    )
    return pl.pallas_call(
        functools.partial(_fox_kernel, hd=hd),
        grid_spec=grid_spec,
        out_shape=jax.ShapeDtypeStruct((n, fw), BF16),
        compiler_params=_params("arbitrary", "arbitrary", "arbitrary"),
        name="fox_attn",
    )(cend, qt, *terms, kb, *terms, vt)


def _fox_cached_kernel(q_ref, kn_ref, vn_ref, kc_ref, vc_ref, lcn_ref, lct_ref, lnn_ref, lnt_ref, o_ref,
                       ck_ref, *, hd):
    ln, fw = q_ref.shape
    past = kc_ref.shape[1]
    heads = fw // hd
    lane8 = lax.broadcasted_iota(jnp.int32, (heads, LANES), 1)

    def body(i, carry):
        off = pl.multiple_of(i * LANES, LANES)
        x = _lane_cumsum(lct_ref[:, pl.ds(off, LANES)], lane8) + carry
        ck_ref[:, pl.ds(off, LANES)] = x
        return x[:, LANES - 1:LANES]

    c_last = lax.fori_loop(0, past // LANES, body, jnp.zeros((heads, 1), F32),
                           unroll=min(8, past // LANES))
    r_i = lax.broadcasted_iota(jnp.int32, (ln, ln), 0)
    c_i = lax.broadcasted_iota(jnp.int32, (ln, ln), 1)
    causal = c_i <= r_i
    tril = causal.astype(F32)
    triu = (r_i <= c_i).astype(F32)
    ck_new = c_last + jnp.dot(lnt_ref[...], triu, precision=HIGHEST, preferred_element_type=F32)
    cache_total = jnp.dot(jnp.ones((ln, past), F32), lcn_ref[...], precision=HIGHEST,
                          preferred_element_type=F32)
    cq = cache_total + jnp.dot(tril, lnn_ref[...], precision=HIGHEST, preferred_element_type=F32)

    outs = []
    for h in range(heads):
        feat = slice(h * hd, (h + 1) * hd)
        qh = q_ref[:, feat]
        kct = kc_ref[feat, :].astype(BF16)
        vct = vc_ref[feat, :].astype(BF16)
        cqh = cq[:, h:h + 1]
        s_c = jnp.dot(qh, kct, preferred_element_type=F32) + cqh - ck_ref[h:h + 1, :]
        s_n = lax.dot_general(qh, kn_ref[:, feat], _NT, preferred_element_type=F32) + cqh - ck_new[h:h + 1, :]
        s_n = jnp.where(causal, s_n, -jnp.inf)
        m = jnp.maximum(jnp.max(s_c, axis=-1, keepdims=True), jnp.max(s_n, axis=-1, keepdims=True))
        p_c = jnp.exp(s_c - m)
        p_n = jnp.exp(s_n - m)
        den = jnp.sum(p_c, axis=-1, keepdims=True) + jnp.sum(p_n, axis=-1, keepdims=True)
        acc = (lax.dot_general(p_c.astype(BF16), vct, _NT, preferred_element_type=F32)
               + jnp.dot(p_n.astype(BF16), vn_ref[:, feat], preferred_element_type=F32))
        outs.append(acc / den)
    o_ref[...] = jnp.concatenate(outs, axis=1).astype(BF16)


def _fox_cached(l, qb, kb, vb, cache_k, cache_v, lf_cache_nat, lf_cache_t, lf_new_nat, lf_new_t, nb, ln, hd):
    n, fw = qb.shape
    past = cache_k.shape[3]
    heads = fw // hd
    rows = lambda b: (b, 0)
    cache = lambda b: (l, b, 0, 0)
    act = pl.BlockSpec((ln, fw), rows)
    return pl.pallas_call(
        functools.partial(_fox_cached_kernel, hd=hd),
        grid=(nb,),
        in_specs=[act, act, act,
                  pl.BlockSpec((None, None, fw, past), cache),
                  pl.BlockSpec((None, None, fw, past), cache),
                  pl.BlockSpec((None, None, past, heads), cache),
                  pl.BlockSpec((None, None, heads, past), cache),
                  pl.BlockSpec((None, ln, heads), lambda b: (b, 0, 0)),
                  pl.BlockSpec((None, heads, ln), lambda b: (b, 0, 0))],
        out_specs=act,
        out_shape=jax.ShapeDtypeStruct((n, fw), BF16),
        scratch_shapes=[pltpu.VMEM((heads, past), F32)],
        compiler_params=_params("arbitrary"),
        name="fox_attn_cached",
    )(qb, kb, vb, cache_k, cache_v, lf_cache_nat, lf_cache_t, lf_new_nat, lf_new_t)


def _gla_kernel(q_ref, k_ref, v_ref, g_ref, la_ref, s0_ref, gn_ref, y_ref, sout_ref, st_ref,
                *, chunk, nh, dk, dv):
    j = pl.program_id(1)
    t_rows = q_ref.shape[0]

    @pl.when(j == 0)
    def _():
        for h in range(nh):
            st_ref[h] = s0_ref[h].T

    row = lax.broadcasted_iota(jnp.int32, (chunk, chunk), 0)
    col = lax.broadcasted_iota(jnp.int32, (chunk, chunk), 1)
    causal = col <= row
    tril = causal.astype(F32)
    mid = chunk // 2
    for c in range(t_rows // chunk):
        rows = slice(c * chunk, (c + 1) * chunk)
        for h in range(nh):
            kc = slice(h * dk, (h + 1) * dk)
            vc = slice(h * dv, (h + 1) * dv)
            b = jnp.dot(tril, la_ref[rows, kc], precision=HIGHEST, preferred_element_type=F32)
            b_last = b[chunk - 1:chunk, :]
            b_mid = b[mid:mid + 1, :]
            q = q_ref[rows, kc].astype(F32)
            k = k_ref[rows, kc].astype(F32)
            v = v_ref[rows, vc]
            qi = (q * jnp.exp(b - b_mid)).astype(BF16)
            ki = (k * jnp.exp(b_mid - b)).astype(BF16)
            a = lax.dot_general(qi, ki, _NT, preferred_element_type=F32)
            a = jnp.where(causal, a, 0.0)
            o = jnp.dot(a.astype(BF16), v, preferred_element_type=F32)
            st = st_ref[h]
            qe = (q * jnp.exp(b)).astype(BF16)
            o = o + lax.dot_general(qe, st.astype(BF16), _NT, preferred_element_type=F32)
            kd = (k * jnp.exp(b_last - b)).astype(BF16)
            st_ref[h] = st * jnp.exp(b_last) + lax.dot_general(v, kd, _TN, preferred_element_type=F32)
            ms = jnp.mean(o * o, axis=-1, keepdims=True)
            y = o * lax.rsqrt(ms + LN_EPS) * gn_ref[...]
            y_ref[rows, vc] = (y * g_ref[rows, vc].astype(F32)).astype(BF16)

    @pl.when(j == pl.num_programs(1) - 1)
    def _():
        for h in range(nh):
            sout_ref[h] = st_ref[h].T


def _gla(l, gq, gk, gv, gg, la, s0, gn, nb, seq, s0_layered):
    n, kw = gq.shape
    nh, dk, dv = s0.shape[-3:]
    t_rows = _tile(seq, T_GLA)
    chunk = _tile(t_rows, GLA_CHUNK)
    nt = seq // t_rows
    rows = lambda b, j: (b * nt + j, 0)
    act = pl.BlockSpec((t_rows, kw), rows)
    if s0_layered:
        s0_spec = pl.BlockSpec((None, None, nh, dk, dv), lambda b, j: (l, b, 0, 0, 0))
    else:
        s0_spec = pl.BlockSpec((None, nh, dk, dv), lambda b, j: (b, 0, 0, 0))
    return pl.pallas_call(
        functools.partial(_gla_kernel, chunk=chunk, nh=nh, dk=dk, dv=dv),
        grid=(nb, nt),
        in_specs=[act, act, act, act, act, s0_spec,
                  _resident((None, 1, dv), lambda b, j: (l, 0, 0))],
        out_specs=[act, pl.BlockSpec((None, nh, dk, dv), lambda b, j: (b, 0, 0, 0))],
        out_shape=[jax.ShapeDtypeStruct((n, kw), BF16), jax.ShapeDtypeStruct((nb, nh, dk, dv), F32)],
        scratch_shapes=[pltpu.VMEM((nh, dv, dk), F32)],
        compiler_params=_params("arbitrary", "arbitrary"),
        name="gla",
    )(gq, gk, gv, gg, la, s0, gn)


def _merge_ffn_kernel(x_ref, mod_ref, yf_ref, ys_ref, yg_ref, wm_ref, wb_ref, wo_ref, g1_ref, b1_ref,
                      wu_ref, cw_ref, cb_ref, wd_ref, g_ref, b_ref, prev_ref, o_ref, st_ref,
                      carry_ref, act_ref, *, d, dff, chunk, alpha):
    j = pl.program_id(1)
    tm = x_ref.shape[0]
    first = j == 0
    x0 = x_ref[...]
    hb = (_ln(x0) * (1.0 + mod_ref[:, d:2 * d]) + mod_ref[:, 0:d]).astype(BF16)
    merged = None
    for i, y_ref in enumerate((yf_ref, ys_ref, yg_ref)):
        gate = jax.nn.sigmoid(jnp.dot(hb, wm_ref[:, i * d:(i + 1) * d], preferred_element_type=F32))
        term = gate * jnp.dot(y_ref[...], wb_ref[i], preferred_element_type=F32)
        merged = term if merged is None else merged + term
    mix = jnp.dot(merged.astype(BF16), wo_ref[...], preferred_element_type=F32)
    x = _ln(alpha * x0 + mod_ref[:, 2 * d:3 * d] * mix) * g1_ref[...] + b1_ref[...]

    hb = (_ln(x) * (1.0 + mod_ref[:, 4 * d:5 * d]) + mod_ref[:, 3 * d:4 * d]).astype(BF16)
    for c0 in range(0, dff, chunk):
        cols = slice(c0, c0 + chunk)
        ug = jnp.dot(hb, wu_ref[:, cols], preferred_element_type=F32)
        uv = jnp.dot(hb, wu_ref[:, dff + c0:dff + c0 + chunk], preferred_element_type=F32)
        p0 = jnp.where(first, prev_ref[0:1, cols], carry_ref[SUBLANES - 2:SUBLANES - 1, cols])
        p1 = jnp.where(first, prev_ref[1:2, cols], carry_ref[SUBLANES - 1:SUBLANES, cols])
        ugc = _causal_conv3(ug, p0, p1, cw_ref, cb_ref, cols)
        carry_ref[:, cols] = ug[tm - SUBLANES:tm, :]
        act_ref[:, cols] = (jax.nn.gelu(ugc) * uv).astype(BF16)
    f = jnp.dot(act_ref[...], wd_ref[...], preferred_element_type=F32)
    st_ref[...] = carry_ref[SUBLANES - 2:SUBLANES, :]
    z = alpha * x + mod_ref[:, 5 * d:6 * d] * f
    o_ref[...] = _ln(z) * g_ref[...] + b_ref[...]


def _merge_ffn(l, x2d, mod, y_fox, y_sc, y_gla, wp, prev, nb, seq, alpha):
    n, d = x2d.shape
    bw = y_fox.shape[1]
    dff = wp["w_down"].shape[1]
    tm = _tile(seq, TM_FFN)
    nt = seq // tm
    chunk = _tile(dff, FFN_CHUNK)
    rows = lambda b, j: (b * nt + j, 0)
    wl3 = lambda b, j: (l, 0, 0)
    yspec = pl.BlockSpec((tm, bw), rows)
    st = pl.BlockSpec((None, 2, dff), lambda b, j: (b, 0, 0))
    return pl.pallas_call(
        functools.partial(_merge_ffn_kernel, d=d, dff=dff, chunk=chunk, alpha=alpha),
        grid=(nb, nt),
        in_specs=[pl.BlockSpec((tm, d), rows),
                  pl.BlockSpec((None, 1, 6 * d), lambda b, j: (b, 0, 0)),
                  yspec, yspec, yspec,
                  _resident((None, d, 3 * d), wl3),
                  _resident((None, 3, bw, d), lambda b, j: (l, 0, 0, 0)),
                  _resident((None, d, d), wl3),
                  _resident((None, 1, d), wl3),
                  _resident((None, 1, d), wl3),
                  _resident((None, d, 2 * dff), wl3),
                  _resident((None, 3, dff), wl3),
                  _resident((None, 1, dff), wl3),
                  _resident((None, dff, d), wl3),
                  _resident((None, 1, d), wl3),
                  _resident((None, 1, d), wl3),
                  st],
        out_specs=[pl.BlockSpec((tm, d), rows), st],
        out_shape=[jax.ShapeDtypeStruct((n, d), F32), jax.ShapeDtypeStruct((nb, 2, dff), F32)],
        scratch_shapes=[pltpu.VMEM((SUBLANES, dff), F32), pltpu.VMEM((tm, dff), BF16)],
        compiler_params=_params("arbitrary", "arbitrary"),
        name="merge_ffn",
    )(x2d, mod, y_fox, y_sc, y_gla, wp["w_merge"], wp["w_branch"], wp["w_out"], wp["ln1_g"], wp["ln1_b"],
      wp["w_up"], wp["ffn_conv_w"], wp["ffn_conv_b"], wp["w_down"], wp["ln2_g"], wp["ln2_b"], prev)


def _pack_weights(w_in, b_fox_f, w_gla_up, b_gla_a, gla_norm_g, sc_conv_w, sc_conv_b, w_branch, w_out,
                  ln1_g, ln1_b, w_up, ffn_conv_w, ffn_conv_b, w_down, ln2_g, ln2_b, heads):
    depth, d, _ = w_in.shape
    fw = sc_conv_w.shape[-1]
    rank = w_gla_up.shape[1]
    o = 0
    fox = w_in[:, :, o:o + 3 * fw]; o += 3 * fw
    wff = w_in[:, :, o:o + heads]; o += heads
    sc = w_in[:, :, o:o + 3 * fw]; o += 3 * fw
    gla = w_in[:, :, o:o + 4 * fw]; o += 4 * fw
    glr = w_in[:, :, o:o + rank]; o += rank
    wm = w_in[:, :, o:]
    row = lambda a: a.reshape(depth, 1, a.shape[-1])
    return {
        "fw": fw,
        "w_main": jnp.concatenate([fox, sc, gla], axis=-1).astype(BF16),
        "w_fft": jnp.swapaxes(wff, 1, 2).astype(BF16),
        "w_glr": jnp.pad(glr, ((0, 0), (0, 0), (0, LANES - rank))).astype(BF16),
        "b_fox": b_fox_f.reshape(depth, heads, 1),
        "w_gla_up": jnp.pad(w_gla_up, ((0, 0), (0, LANES - rank), (0, 0))).astype(BF16),
        "b_gla": row(b_gla_a),
        "gla_norm_g": row(gla_norm_g),
        "sc_w": sc_conv_w, "sc_b": row(sc_conv_b),
        "w_merge": wm.astype(BF16),
        "w_branch": w_branch.astype(BF16),
        "w_out": w_out.astype(BF16),
        "ln1_g": row(ln1_g), "ln1_b": row(ln1_b),
        "w_up": w_up.astype(BF16),
        "ffn_conv_w": ffn_conv_w, "ffn_conv_b": row(ffn_conv_b),
        "w_down": w_down.astype(BF16),
        "ln2_g": row(ln2_g), "ln2_b": row(ln2_b),
    }


def _layer(l, x2d, mod, wp, nb, seq, hd, alpha, fox_cache, sc_prev, gla_s0, ffn_prev, kf_all, vf_all):
    fw = wp["fw"]
    heads = fw // hd
    dk = gla_s0.shape[-2]
    prompt = fox_cache is None
    (q, kf_all, vf_all, kb, v, lft, y_sc, sc_new, gq, gk, gv, gg, la, *kn2) = _inproj(
        l, x2d, mod, wp, sc_prev, kf_all, vf_all, nb, seq, hd, dk, prompt)
    lf_t = lft.reshape(nb, -1, heads, lft.shape[-1]).transpose(2, 0, 1, 3).reshape(heads, nb, seq)
    logf = lf_t.transpose(1, 2, 0)
    if prompt:
        cp, c1, c2, c3 = _cumsum_rows(lf_t.reshape(heads * nb, seq))
        knorm = jnp.sqrt(jnp.max(kn2[0][:, 0, :heads].reshape(nb, -1, heads), axis=1))
        y_fox = _fox_prompt(cp, c1, c2, c3, knorm, q, kb, v, nb, seq, hd)
        y_gla, gla_new = _gla(l, gq, gk, gv, gg, la, gla_s0, wp["gla_norm_g"], nb, seq, False)
    else:
        cache_k, cache_v, lf_cache_nat, lf_cache_t = fox_cache
        y_fox = _fox_cached(l, q, kb, v, cache_k, cache_v, lf_cache_nat, lf_cache_t,
                            logf, lf_t.transpose(1, 0, 2), nb, seq, hd)
        y_gla, gla_new = _gla(l, gq, gk, gv, gg, la, gla_s0, wp["gla_norm_g"], nb, seq, True)
    x2, ffn_new = _merge_ffn(l, x2d, mod, y_fox, y_sc, y_gla, wp, ffn_prev, nb, seq, alpha)
    return x2, kf_all, vf_all, (logf, sc_new, gla_new, ffn_new)


def kernel(x_prompt, x_sample, c_prompt, c_sample, cache_fox_k, cache_fox_v, cache_fox_logf, state_shortconv, state_gla, state_ffn_conv, w_ada, b_ada, w_in, b_fox_f, w_gla_up, b_gla_a, gla_norm_g, sc_conv_w, sc_conv_b, w_branch, w_out, ln1_g, ln1_b, w_up, ffn_conv_w, ffn_conv_b, w_down, ln2_g, ln2_b):
    bp, sp, d = x_prompt.shape
    bs, ss, _ = x_sample.shape
    depth = w_in.shape[0]
    _, _, past, heads, hd = cache_fox_k.shape
    fw = heads * hd
    dff = ffn_conv_w.shape[-1]
    assert sc_conv_w.shape[-1] == fw and state_gla.shape[2] * state_gla.shape[3] == fw
    assert state_gla.shape[2] * state_gla.shape[4] == fw and fw % LANES == 0 and 2 * hd == LANES
    assert sc_conv_w.shape[1] == 3 and ffn_conv_w.shape[1] == 3
    alpha = (2.0 * depth) ** 0.25

    wp = _pack_weights(w_in, b_fox_f, w_gla_up, b_gla_a, gla_norm_g, sc_conv_w, sc_conv_b, w_branch, w_out,
                       ln1_g, ln1_b, w_up, ffn_conv_w, ffn_conv_b, w_down, ln2_g, ln2_b, heads)
    mods = _ada(jnp.concatenate([c_prompt, c_sample], axis=0), w_ada, b_ada)
    cache_k = cache_fox_k.transpose(0, 1, 3, 4, 2).reshape(depth, bs, fw, past)
    cache_v = cache_fox_v.transpose(0, 1, 3, 4, 2).reshape(depth, bs, fw, past)
    lf_cache_t = jnp.swapaxes(cache_fox_logf, 2, 3)

    xp = x_prompt.reshape(bp * sp, d)
    xs = x_sample.reshape(bs * ss, d)
    zeros_sc = jnp.zeros((bp, 2, fw), F32)
    zeros_gla = jnp.zeros((bp,) + state_gla.shape[2:], F32)
    zeros_ffn = jnp.zeros((bp, 2, dff), F32)
    kp_all = jnp.zeros((depth, bp, fw, sp), F32)
    vp_all = jnp.zeros((depth, bp, fw, sp), F32)
    ks_all = jnp.zeros((depth, bs * ss, fw), F32)
    vs_all = jnp.zeros((depth, bs * ss, fw), F32)
    outs_p = [[] for _ in range(4)]
    outs_s = [[] for _ in range(4)]
    for l in range(depth):
        mod_p = mods[l, :bp].reshape(bp, 1, 6 * d)
        mod_s = mods[l, bp:].reshape(bs, 1, 6 * d)
        xp, kp_all, vp_all, st_p = _layer(l, xp, mod_p, wp, bp, sp, hd, alpha, None,
                                          zeros_sc, zeros_gla, zeros_ffn, kp_all, vp_all)
        xs, ks_all, vs_all, st_s = _layer(l, xs, mod_s, wp, bs, ss, hd, alpha,
                                          (cache_k, cache_v, cache_fox_logf, lf_cache_t),
                                          state_shortconv[l], state_gla, state_ffn_conv[l], ks_all, vs_all)
        for i in range(4):
            outs_p[i].append(st_p[i])
            outs_s[i].append(st_s[i])
    stacked_p = [jnp.stack(o) for o in outs_p]
    stacked_s = [jnp.stack(o) for o in outs_s]
    kv_p = lambda a: a.reshape(depth, bp, heads, hd, sp).transpose(0, 1, 4, 2, 3)
    kv_s = lambda a: a.reshape(depth, bs, ss, heads, hd)
    return (xp.reshape(bp, sp, d), xs.reshape(bs, ss, d),
            kv_p(kp_all), kv_p(vp_all), *stacked_p,
            kv_s(ks_all), kv_s(vs_all), *stacked_s)
```

```python
import functools

import jax
import jax.numpy as jnp
from jax import lax
from jax.experimental import pallas as pl
from jax.experimental.pallas import tpu as pltpu

F32 = jnp.float32
BF16 = jnp.bfloat16
LN_EPS = 1e-5
GLA_TAU = 16.0
LANES = 128
SUBLANES = 8
VMEM_LIMIT = 56 * 1024 * 1024
HIGHEST = lax.Precision.HIGHEST
LOG2E = 1.4426950408889634
FOX_UNDERFLOW = 140.0
FOX_NORM_MARGIN = 1.01

TM_PROJ = 512
TM_FFN = 512
TQ_FOX = 512
T_GLA = 1024
GLA_CHUNK = 128
FFN_CHUNK = 256
ADA_TN = 1024

_NT = (((1,), (1,)), ((), ()))
_TN = (((0,), (0,)), ((), ()))


def _tile(n, pref):
    if n <= pref:
        return n
    t = pref
    while n % t:
        t //= 2
    return t


def _params(*sem):
    return pltpu.CompilerParams(dimension_semantics=sem, vmem_limit_bytes=VMEM_LIMIT)


def _resident(block, index_map):
    return pl.BlockSpec(block, index_map, pipeline_mode=pl.Buffered(1))


def _log_sigmoid(x):
    return jnp.minimum(x, 0.0) - jnp.log1p(jnp.exp(-jnp.abs(x)))


def _ln(x):
    mu = jnp.mean(x, axis=-1, keepdims=True)
    xc = x - mu
    var = jnp.mean(xc * xc, axis=-1, keepdims=True)
    return xc * lax.rsqrt(var + LN_EPS)


def _causal_conv3(u, p0, p1, w_ref, b_ref, cols):
    tm = u.shape[0]
    w0, w1, w2, bias = w_ref[0:1, cols], w_ref[1:2, cols], w_ref[2:3, cols], b_ref[:, cols]

    def taps(u0, u1, u2):
        y = bias + u2 * w0
        y = y + u1 * w1
        return y + u0 * w2

    head = u[0:SUBLANES, :]
    row = lax.broadcasted_iota(jnp.int32, (SUBLANES, 1), 0)
    h1 = jnp.where(row == 0, p1, pltpu.roll(head, 1, 0))
    h2 = jnp.where(row == 0, p0, jnp.where(row == 1, p1, pltpu.roll(head, 2, 0)))
    first = taps(head, h1, h2)
    if tm == SUBLANES:
        return first
    rest = taps(u, pltpu.roll(u, 1, 0), pltpu.roll(u, 2, 0))
    return jnp.concatenate([first, rest[SUBLANES:, :]], axis=0)


def _ada_kernel(c_ref, w_ref, b_ref, o_ref):
    c = c_ref[...]
    s = (c * jax.nn.sigmoid(c)).astype(BF16)
    o_ref[...] = jnp.dot(s, w_ref[...].astype(BF16), preferred_element_type=F32) + b_ref[...]


def _ada(c_all, w_ada, b_ada):
    depth, d, n6 = w_ada.shape
    r = c_all.shape[0]
    tn = _tile(n6, ADA_TN)
    return pl.pallas_call(
        _ada_kernel,
        grid=(depth, n6 // tn),
        in_specs=[pl.BlockSpec((r, d), lambda l, n: (0, 0)),
                  pl.BlockSpec((None, d, tn), lambda l, n: (l, 0, n)),
                  pl.BlockSpec((None, 1, tn), lambda l, n: (l, 0, n))],
        out_specs=pl.BlockSpec((None, r, tn), lambda l, n: (l, 0, n)),
        out_shape=jax.ShapeDtypeStruct((depth, r, n6), F32),
        compiler_params=_params("arbitrary", "arbitrary"),
        name="ada_mod",
    )(c_all, w_ada, b_ada.reshape(depth, 1, n6))


def _inproj_kernel(x_ref, mod_ref, w_ref, wff_ref, wglr_ref, bff_ref, wup_ref, ba_ref, scw_ref, scb_ref,
                   prev_ref, kf_all_ref, vf_all_ref,
                   q_ref, kf_ref, vf_ref, kb_ref, vb_ref, lf_ref, ysc_ref, scst_ref,
                   gq_ref, gk_ref, gv_ref, gg_ref, la_ref,
                   *tail, d, fw, hd, q_scale, gq_scale, transposed):
    del kf_all_ref, vf_all_ref
    carry_ref = tail[-1]
    j = pl.program_id(1)
    tm = x_ref.shape[0]
    h = _ln(x_ref[...]) * (1.0 + mod_ref[:, d:2 * d]) + mod_ref[:, 0:d]
    hb = h.astype(BF16)

    def proj(g):
        return jnp.dot(hb, w_ref[:, g * fw:(g + 1) * fw], preferred_element_type=F32)

    q = proj(0) * q_scale
    k = proj(1)
    v = proj(2)
    kb_ref[...] = k.astype(BF16)
    if transposed:
        vt = v.T
        q_ref[...] = q.T.astype(BF16)
        kf_ref[...] = k.T
        vf_ref[...] = vt
        vb_ref[...] = vt.astype(BF16)
        kk = k * k
        lane = lax.broadcasted_iota(jnp.int32, (1, LANES), 1)
        norms = jnp.zeros((1, LANES), F32)
        for hh in range(fw // hd):
            n2 = jnp.sum(kk[:, hh * hd:(hh + 1) * hd], axis=-1, keepdims=True)
            norms = jnp.where(lane == hh, jnp.max(n2, axis=0, keepdims=True), norms)
        tail[0][...] = norms
    else:
        q_ref[...] = q.astype(BF16)
        kf_ref[...] = k
        vf_ref[...] = v
        vb_ref[...] = v.astype(BF16)
    fft =lax.dot_general(wff_ref[...], hb, _NT, preferred_element_type=F32)
    lf_ref[...] = _log_sigmoid(fft + bff_ref[...])

    first = j == 0
    p0 = jnp.where(first, prev_ref[0:1, :], carry_ref[SUBLANES - 2:SUBLANES - 1, :])
    p1 = jnp.where(first, prev_ref[1:2, :], carry_ref[SUBLANES - 1:SUBLANES, :])
    sb = proj(3)
    u = proj(4) * proj(5)
    y = _causal_conv3(u, p0, p1, scw_ref, scb_ref, slice(None))
    ysc_ref[...] = (sb * y).astype(BF16)
    carry_ref[...] = u[tm - SUBLANES:tm, :]
    scst_ref[...] = carry_ref[SUBLANES - 2:SUBLANES, :]

    gq_ref[...] = (proj(6) * gq_scale).astype(BF16)
    gk_ref[...] = proj(7).astype(BF16)
    gv_ref[...] = proj(8).astype(BF16)
    gg = proj(9)
    gg_ref[...] = (gg * jax.nn.sigmoid(gg)).astype(BF16)
    glr = jnp.dot(hb, wglr_ref[...], preferred_element_type=F32)
    lap = jnp.dot(glr.astype(BF16), wup_ref[...], preferred_element_type=F32) + ba_ref[...]
    la_ref[...] = _log_sigmoid(lap) / GLA_TAU


def _inproj(l, x2d, mod, wp, sc_prev, kf_all, vf_all, nb, seq, hd, dk, transposed):
    n, d = x2d.shape
    fw = wp["fw"]
    heads = fw // hd
    tm = _tile(seq, TM_PROJ)
    nt = seq // tm
    rows = lambda b, j: (b * nt + j, 0)
    wl = lambda b, j: (l, 0, 0)
    act = lambda dt: jax.ShapeDtypeStruct((n, fw), dt)
    act_spec = pl.BlockSpec((tm, fw), rows)
    if transposed:
        t_shape, t_spec = jax.ShapeDtypeStruct((fw, n), BF16), pl.BlockSpec((fw, tm), lambda b, j: (0, b * nt + j))
        q_scale = LOG2E * hd ** -0.5
        stacked = pl.BlockSpec((None, None, fw, tm), lambda b, j: (l, b, 0, j))
    else:
        t_shape, t_spec, q_scale = act(BF16), act_spec, hd ** -0.5
        stacked = pl.BlockSpec((None, tm, fw), lambda b, j: (l, b * nt + j, 0))
    hbm = pl.BlockSpec(memory_space=pl.ANY)
    kern = functools.partial(_inproj_kernel, d=d, fw=fw, hd=hd, q_scale=q_scale, gq_scale=dk ** -0.5,
                             transposed=transposed)
    extra_specs = [pl.BlockSpec((None, 1, LANES), lambda b, j: (b * nt + j, 0, 0))] if transposed else []
    extra_shapes = [jax.ShapeDtypeStruct((nb * nt, 1, LANES), F32)] if transposed else []
    return pl.pallas_call(
        kern,
        grid=(nb, nt),
        in_specs=[pl.BlockSpec((tm, d), rows),
                  pl.BlockSpec((None, 1, 6 * d), lambda b, j: (b, 0, 0)),
                  _resident((None, d, 10 * fw), wl),
                  _resident((None, heads, d), wl),
                  _resident((None, d, LANES), wl),
                  _resident((None, heads, 1), wl),
                  _resident((None, LANES, fw), wl),
                  _resident((None, 1, fw), wl),
                  _resident((None, 3, fw), wl),
                  _resident((None, 1, fw), wl),
                  pl.BlockSpec((None, 2, fw), lambda b, j: (b, 0, 0)),
                  hbm, hbm],
        out_specs=[t_spec, stacked, stacked, act_spec, t_spec,
                   pl.BlockSpec((None, heads, tm), lambda b, j: (b * nt + j, 0, 0)),
                   act_spec,
                   pl.BlockSpec((None, 2, fw), lambda b, j: (b, 0, 0)),
                   act_spec, act_spec, act_spec, act_spec, act_spec] + extra_specs,
        out_shape=[t_shape, jax.ShapeDtypeStruct(kf_all.shape, F32), jax.ShapeDtypeStruct(vf_all.shape, F32),
                   act(BF16), t_shape,
                   jax.ShapeDtypeStruct((nb * nt, heads, tm), F32),
                   act(BF16),
                   jax.ShapeDtypeStruct((nb, 2, fw), F32),
                   act(BF16), act(BF16), act(BF16), act(BF16), act(F32)] + extra_shapes,
        input_output_aliases={11: 1, 12: 2},
        scratch_shapes=[pltpu.VMEM((SUBLANES, fw), F32)],
        compiler_params=_params("arbitrary", "arbitrary"),
        name="in_proj",
    )(x2d, mod, wp["w_main"], wp["w_fft"], wp["w_glr"], wp["b_fox"], wp["w_gla_up"], wp["b_gla"],
      wp["sc_w"], wp["sc_b"], sc_prev, kf_all, vf_all)


def _lane_cumsum(x, lane):
    for k in range(7):
        s = 1 << k
        x = x + jnp.where(lane >= s, pltpu.roll(x, s, 1), 0.0)
    return x


def _cumsum_kernel(x_ref, cp_ref, c1_ref, c2_ref, c3_ref):
    r, length = x_ref.shape
    lane = lax.broadcasted_iota(jnp.int32, (r, LANES), 1)

    def body(i, carry):
        off = pl.multiple_of(i * LANES, LANES)
        x = _lane_cumsum(x_ref[:, pl.ds(off, LANES)], lane) + carry
        c = x * LOG2E
        c1 = c.astype(BF16).astype(F32)
        r1 = c - c1
        c2 = r1.astype(BF16).astype(F32)
        c3 = (r1 - c2).astype(BF16).astype(F32)
        cp_ref[:, pl.ds(off, LANES)] = c
        c1_ref[:, pl.ds(off, LANES)] = c1
        c2_ref[:, pl.ds(off, LANES)] = c2
        c3_ref[:, pl.ds(off, LANES)] = c3
        return x[:, LANES - 1:LANES]

    lax.fori_loop(0, length // LANES, body, jnp.zeros((r, 1), F32), unroll=min(8, length // LANES))


def _cumsum_rows(x):
    r, length = x.shape
    spec = pl.BlockSpec((r, length), lambda i: (0, 0))
    return pl.pallas_call(
        _cumsum_kernel,
        grid=(1,),
        in_specs=[spec],
        out_specs=[spec, spec, spec, spec],
        out_shape=[jax.ShapeDtypeStruct((r, length), F32)] * 4,
        compiler_params=_params("arbitrary"),
        name="fox_cumsum",
    )(x)


def _fox_kernel(cend_ref, qt_ref, cq1_ref, cq2_ref, cq3_ref, k_ref, ck1_ref, ck2_ref, ck3_ref, vt_ref, o_ref,
                m_ref, acc_ref, qs_ref, ak_ref, *, hd):
    b = pl.program_id(0)
    pr = pl.program_id(1)
    i = pl.program_id(2)
    tq = qt_ref.shape[1]
    seq = k_ref.shape[0]
    sub = tq // 2
    nk = seq // sub
    lane = lax.broadcasted_iota(jnp.int32, (1, LANES), 1)
    head_a = lane < hd
    cqs = (cq1_ref, cq2_ref, cq3_ref)
    cks = (ck1_ref, ck2_ref, ck3_ref)

    group = 2 * SUBLANES

    @pl.when(i == 0)
    def _():
        rk = lax.broadcasted_iota(jnp.int32, (group, LANES), 0)
        gap = jnp.zeros((hd - group, LANES), F32)

        def chunk(t, carry):
            off = pl.multiple_of(t * LANES, LANES)
            parts = []
            for h in (1, 0):
                rows = jnp.where(rk < 3, 1.0, 0.0)
                for n, ck in enumerate(cks):
                    rows = jnp.where(rk == 3 + n, -ck[h, :, pl.ds(off, LANES)], rows)
                parts += [rows, gap]
            ak_ref[pl.ds(off, LANES), :] = jnp.concatenate(parts, axis=0).T.astype(BF16)
            return carry

        lax.fori_loop(0, seq // LANES, chunk, 0, unroll=min(4, seq // LANES))

    rg = lax.broadcasted_iota(jnp.int32, (group, tq), 0)

    def bias_rows(h):
        rows = jnp.where(rg < 6, 1.0, 0.0)
        for n, cq in enumerate(cqs):
            rows = jnp.where(rg == n, cq[h], rows)
        return rows.astype(BF16)

    qs_ref[0, 0:hd, :] = qt_ref[0:hd, :]
    qs_ref[0, hd:hd + group, :] = bias_rows(0)
    qs_ref[0, hd + group:, :] = jnp.zeros((hd - group, tq), BF16)
    qs_ref[1, 0:group, :] = bias_rows(1)
    qs_ref[1, group:hd, :] = jnp.zeros((hd - group, tq), BF16)
    qs_ref[1, hd:, :] = qt_ref[hd:, :]
    m_ref[...] = jnp.full(m_ref.shape, -jnp.inf, F32)
    acc_ref[...] = jnp.zeros(acc_ref.shape, F32)

    def step(jk, tk, diagonal):
        off = pl.multiple_of(jk * sub, sub)
        ones = jnp.ones((hd, tk), BF16)
        k = k_ref[pl.ds(off, tk), :]
        ak = ak_ref[pl.ds(off, tk), :]
        ks = (jnp.where(head_a, k, ak), jnp.where(head_a, ak, k))
        vts = (jnp.concatenate([vt_ref[0:hd, pl.ds(off, tk)], ones], axis=0),
               jnp.concatenate([ones, vt_ref[hd:, pl.ds(off, tk)]], axis=0))
        sts = [jnp.dot(ks[h], qs_ref[h], preferred_element_type=F32) for h in range(2)]
        if diagonal:
            krow = lax.broadcasted_iota(jnp.int32, (tk, tq), 0)
            qcol = lax.broadcasted_iota(jnp.int32, (tk, tq), 1)
            sts = [jnp.where(krow <= qcol, st, -jnp.inf) for st in sts]
        m_prevs = [m_ref[h] for h in range(2)]
        m_news = [jnp.maximum(m_prevs[h], jnp.max(sts[h], axis=0, keepdims=True)) for h in range(2)]
        for h in range(2):
            p = jnp.exp2(sts[h] - m_news[h]).astype(BF16)
            alpha = jnp.exp2(m_prevs[h] - m_news[h])
            acc_ref[h] = alpha * acc_ref[h] + jnp.dot(vts[h], p, preferred_element_type=F32)
            m_ref[h] = m_news[h]

    first = 2 * i
    step(first, tq, True)

    pair = b * pl.num_programs(1) + pr
    base = pair * 2 * nk
    kmax_base = pl.num_programs(0) * pl.num_programs(1) * 2 * nk + pair * 2
    slack = []
    for h in range(2):
        qf = qt_ref[h * hd:(h + 1) * hd, :].astype(F32)
        qn = jnp.sqrt(jnp.max(jnp.sum(qf * qf, axis=0, keepdims=True)))
        slack.append(qn * cend_ref[kmax_base + h] * FOX_NORM_MARGIN - jnp.min(m_ref[h]))
    before = jnp.maximum(first - 1, 0)

    def needed(j):
        r = False
        for h in range(2):
            gap = cend_ref[base + h * nk + before] - cend_ref[base + h * nk + j]
            r = jnp.logical_or(r, slack[h] + gap > -FOX_UNDERFLOW)
        return r

    j_lo = lax.while_loop(lambda j: jnp.logical_and(j > 0, needed(jnp.maximum(j - 1, 0))), lambda j: j - 1,
                          first)
    n_off = first - j_lo
    rem = n_off % 4
    for r in (1, 2, 3):
        @pl.when(rem == r)
        def _(r=r):
            step(j_lo, r * sub, False)

    def body(t, carry):
        step(j_lo + rem + 4 * t, 4 * sub, False)
        return carry

    lax.fori_loop(0, n_off // 4, body, 0)

    acc_a = acc_ref[0]
    acc_b = acc_ref[1]
    out_t = jnp.concatenate([acc_a[0:hd] / acc_a[hd:hd + 1], acc_b[hd:] / acc_b[0:1]], axis=0)
    o_ref[...] = out_t.T.astype(BF16)


def _fox_prompt(cp, c1, c2, c3, knorm, qt, kb, vt, nb, seq, hd):
    fw, n = qt.shape
    pairs = fw // LANES
    tq = _tile(seq, TQ_FOX)
    nq = seq // tq
    by_pair = lambda a: a.reshape(pairs, 2, nb, 1, seq)
    sub = tq // 2
    cend = by_pair(cp)[:, :, :, 0, sub - 1::sub].transpose(2, 0, 1, 3).reshape(-1)
    cend = jnp.concatenate([cend, knorm.reshape(-1)])
    terms = [by_pair(c) for c in (c1, c2, c3)]
    qmap = lambda b, p, i, c: (p, b * nq + i)
    kmap = lambda b, p, i, c: (b, p)
    cq_spec = pl.BlockSpec((None, 2, None, 1, tq), lambda b, p, i, c: (p, 0, b, 0, i))
    ck_spec = pl.BlockSpec((None, 2, None, 1, seq), lambda b, p, i, c: (p, 0, b, 0, 0))
    grid_spec = pltpu.PrefetchScalarGridSpec(
        num_scalar_prefetch=1,
        grid=(nb, pairs, nq),
        in_specs=[pl.BlockSpec((LANES, tq), qmap), cq_spec, cq_spec, cq_spec,
                  pl.BlockSpec((seq, LANES), kmap), ck_spec, ck_spec, ck_spec,
                  pl.BlockSpec((LANES, seq), lambda b, p, i, c: (p, b))],
        out_specs=pl.BlockSpec((tq, LANES), lambda b, p, i, c: (b * nq + i, p)),
        scratch_shapes=[pltpu.VMEM((2, 1, tq), F32), pltpu.VMEM((2, LANES, tq), F32),
                        pltpu.VMEM((2, LANES, tq), BF16), pltpu.VMEM((seq, LANES), BF16)],
    )
    return pl.pallas_call(
        functools.partial(_fox_kernel, hd=hd),
        grid_spec=grid_spec,
        out_shape=jax.ShapeDtypeStruct((n, fw), BF16),
        compiler_params=_params("arbitrary", "arbitrary", "arbitrary"),
        name="fox_attn",
    )(cend, qt, *terms, kb, *terms, vt)


def _fox_cached_kernel(q_ref, kn_ref, vn_ref, kc_ref, vc_ref, lcn_ref, lct_ref, lnn_ref, lnt_ref, o_ref,
                       ck_ref, *, hd):
    ln, fw = q_ref.shape
    past = kc_ref.shape[1]
    heads = fw // hd
    lane8 = lax.broadcasted_iota(jnp.int32, (heads, LANES), 1)

    def body(i, carry):
        off = pl.multiple_of(i * LANES, LANES)
        x = _lane_cumsum(lct_ref[:, pl.ds(off, LANES)], lane8) + carry
        ck_ref[:, pl.ds(off, LANES)] = x
        return x[:, LANES - 1:LANES]

    c_last = lax.fori_loop(0, past // LANES, body, jnp.zeros((heads, 1), F32),
                           unroll=min(8, past // LANES))
    r_i = lax.broadcasted_iota(jnp.int32, (ln, ln), 0)
    c_i = lax.broadcasted_iota(jnp.int32, (ln, ln), 1)
    causal = c_i <= r_i
    tril = causal.astype(F32)
    triu = (r_i <= c_i).astype(F32)
    ck_new = c_last + jnp.dot(lnt_ref[...], triu, precision=HIGHEST, preferred_element_type=F32)
    cache_total = jnp.dot(jnp.ones((ln, past), F32), lcn_ref[...], precision=HIGHEST,
                          preferred_element_type=F32)
    cq = cache_total + jnp.dot(tril, lnn_ref[...], precision=HIGHEST, preferred_element_type=F32)

    outs = []
    for h in range(heads):
        feat = slice(h * hd, (h + 1) * hd)
        qh = q_ref[:, feat]
        kct = kc_ref[feat, :].astype(BF16)
        vct = vc_ref[feat, :].astype(BF16)
        cqh = cq[:, h:h + 1]
        s_c = jnp.dot(qh, kct, preferred_element_type=F32) + cqh - ck_ref[h:h + 1, :]
        s_n = lax.dot_general(qh, kn_ref[:, feat], _NT, preferred_element_type=F32) + cqh - ck_new[h:h + 1, :]
        s_n = jnp.where(causal, s_n, -jnp.inf)
        m = jnp.maximum(jnp.max(s_c, axis=-1, keepdims=True), jnp.max(s_n, axis=-1, keepdims=True))
        p_c = jnp.exp(s_c - m)
        p_n = jnp.exp(s_n - m)
        den = jnp.sum(p_c, axis=-1, keepdims=True) + jnp.sum(p_n, axis=-1, keepdims=True)
        acc = (lax.dot_general(p_c.astype(BF16), vct, _NT, preferred_element_type=F32)
               + jnp.dot(p_n.astype(BF16), vn_ref[:, feat], preferred_element_type=F32))
        outs.append(acc / den)
    o_ref[...] = jnp.concatenate(outs, axis=1).astype(BF16)


def _fox_cached(l, qb, kb, vb, cache_k, cache_v, lf_cache_nat, lf_cache_t, lf_new_nat, lf_new_t, nb, ln, hd):
    n, fw = qb.shape
    past = cache_k.shape[3]
    heads = fw // hd
    rows = lambda b: (b, 0)
    cache = lambda b: (l, b, 0, 0)
    act = pl.BlockSpec((ln, fw), rows)
    return pl.pallas_call(
        functools.partial(_fox_cached_kernel, hd=hd),
        grid=(nb,),
        in_specs=[act, act, act,
                  pl.BlockSpec((None, None, fw, past), cache),
                  pl.BlockSpec((None, None, fw, past), cache),
                  pl.BlockSpec((None, None, past, heads), cache),
                  pl.BlockSpec((None, None, heads, past), cache),
                  pl.BlockSpec((None, ln, heads), lambda b: (b, 0, 0)),
                  pl.BlockSpec((None, heads, ln), lambda b: (b, 0, 0))],
        out_specs=act,
        out_shape=jax.ShapeDtypeStruct((n, fw), BF16),
        scratch_shapes=[pltpu.VMEM((heads, past), F32)],
        compiler_params=_params("arbitrary"),
        name="fox_attn_cached",
    )(qb, kb, vb, cache_k, cache_v, lf_cache_nat, lf_cache_t, lf_new_nat, lf_new_t)


def _gla_kernel(q_ref, k_ref, v_ref, g_ref, la_ref, s0_ref, gn_ref, y_ref, sout_ref, st_ref,
                *, chunk, nh, dk, dv):
    j = pl.program_id(1)
    t_rows = q_ref.shape[0]

    @pl.when(j == 0)
    def _():
        for h in range(nh):
            st_ref[h] = s0_ref[h].T

    row = lax.broadcasted_iota(jnp.int32, (chunk, chunk), 0)
    col = lax.broadcasted_iota(jnp.int32, (chunk, chunk), 1)
    causal = col <= row
    tril = causal.astype(F32)
    mid = chunk // 2
    for c in range(t_rows // chunk):
        rows = slice(c * chunk, (c + 1) * chunk)
        for h in range(nh):
            kc = slice(h * dk, (h + 1) * dk)
            vc = slice(h * dv, (h + 1) * dv)
            b = jnp.dot(tril, la_ref[rows, kc], precision=HIGHEST, preferred_element_type=F32)
            b_last = b[chunk - 1:chunk, :]
            b_mid = b[mid:mid + 1, :]
            q = q_ref[rows, kc].astype(F32)
            k = k_ref[rows, kc].astype(F32)
            v = v_ref[rows, vc]
            qi = (q * jnp.exp(b - b_mid)).astype(BF16)
            ki = (k * jnp.exp(b_mid - b)).astype(BF16)
            a = lax.dot_general(qi, ki, _NT, preferred_element_type=F32)
            a = jnp.where(causal, a, 0.0)
            o = jnp.dot(a.astype(BF16), v, preferred_element_type=F32)
            st = st_ref[h]
            qe = (q * jnp.exp(b)).astype(BF16)
            o = o + lax.dot_general(qe, st.astype(BF16), _NT, preferred_element_type=F32)
            kd = (k * jnp.exp(b_last - b)).astype(BF16)
            st_ref[h] = st * jnp.exp(b_last) + lax.dot_general(v, kd, _TN, preferred_element_type=F32)
            ms = jnp.mean(o * o, axis=-1, keepdims=True)
            y = o * lax.rsqrt(ms + LN_EPS) * gn_ref[...]
            y_ref[rows, vc] = (y * g_ref[rows, vc].astype(F32)).astype(BF16)

    @pl.when(j == pl.num_programs(1) - 1)
    def _():
        for h in range(nh):
            sout_ref[h] = st_ref[h].T


def _gla(l, gq, gk, gv, gg, la, s0, gn, nb, seq, s0_layered):
    n, kw = gq.shape
    nh, dk, dv = s0.shape[-3:]
    t_rows = _tile(seq, T_GLA)
    chunk = _tile(t_rows, GLA_CHUNK)
    nt = seq // t_rows
    rows = lambda b, j: (b * nt + j, 0)
    act = pl.BlockSpec((t_rows, kw), rows)
    if s0_layered:
        s0_spec = pl.BlockSpec((None, None, nh, dk, dv), lambda b, j: (l, b, 0, 0, 0))
    else:
        s0_spec = pl.BlockSpec((None, nh, dk, dv), lambda b, j: (b, 0, 0, 0))
    return pl.pallas_call(
        functools.partial(_gla_kernel, chunk=chunk, nh=nh, dk=dk, dv=dv),
        grid=(nb, nt),
        in_specs=[act, act, act, act, act, s0_spec,
                  _resident((None, 1, dv), lambda b, j: (l, 0, 0))],
        out_specs=[act, pl.BlockSpec((None, nh, dk, dv), lambda b, j: (b, 0, 0, 0))],
        out_shape=[jax.ShapeDtypeStruct((n, kw), BF16), jax.ShapeDtypeStruct((nb, nh, dk, dv), F32)],
        scratch_shapes=[pltpu.VMEM((nh, dv, dk), F32)],
        compiler_params=_params("arbitrary", "arbitrary"),
        name="gla",
    )(gq, gk, gv, gg, la, s0, gn)


def _merge_ffn_kernel(x_ref, mod_ref, yf_ref, ys_ref, yg_ref, wm_ref, wb_ref, wo_ref, g1_ref, b1_ref,
                      wu_ref, cw_ref, cb_ref, wd_ref, g_ref, b_ref, prev_ref, o_ref, st_ref,
                      carry_ref, act_ref, *, d, dff, chunk, alpha):
    j = pl.program_id(1)
    tm = x_ref.shape[0]
    first = j == 0
    x0 = x_ref[...]
    hb = (_ln(x0) * (1.0 + mod_ref[:, d:2 * d]) + mod_ref[:, 0:d]).astype(BF16)
    merged = None
    for i, y_ref in enumerate((yf_ref, ys_ref, yg_ref)):
        gate = jax.nn.sigmoid(jnp.dot(hb, wm_ref[:, i * d:(i + 1) * d], preferred_element_type=F32))
        term = gate * jnp.dot(y_ref[...], wb_ref[i], preferred_element_type=F32)
        merged = term if merged is None else merged + term
    mix = jnp.dot(merged.astype(BF16), wo_ref[...], preferred_element_type=F32)
    x = _ln(alpha * x0 + mod_ref[:, 2 * d:3 * d] * mix) * g1_ref[...] + b1_ref[...]

    hb = (_ln(x) * (1.0 + mod_ref[:, 4 * d:5 * d]) + mod_ref[:, 3 * d:4 * d]).astype(BF16)
    for c0 in range(0, dff, chunk):
        cols = slice(c0, c0 + chunk)
        ug = jnp.dot(hb, wu_ref[:, cols], preferred_element_type=F32)
        uv = jnp.dot(hb, wu_ref[:, dff + c0:dff + c0 + chunk], preferred_element_type=F32)
        p0 = jnp.where(first, prev_ref[0:1, cols], carry_ref[SUBLANES - 2:SUBLANES - 1, cols])
        p1 = jnp.where(first, prev_ref[1:2, cols], carry_ref[SUBLANES - 1:SUBLANES, cols])
        ugc = _causal_conv3(ug, p0, p1, cw_ref, cb_ref, cols)
        carry_ref[:, cols] = ug[tm - SUBLANES:tm, :]
        act_ref[:, cols] = (jax.nn.gelu(ugc) * uv).astype(BF16)
    f = jnp.dot(act_ref[...], wd_ref[...], preferred_element_type=F32)
    st_ref[...] = carry_ref[SUBLANES - 2:SUBLANES, :]
    z = alpha * x + mod_ref[:, 5 * d:6 * d] * f
    o_ref[...] = _ln(z) * g_ref[...] + b_ref[...]


def _merge_ffn(l, x2d, mod, y_fox, y_sc, y_gla, wp, prev, nb, seq, alpha):
    n, d = x2d.shape
    bw = y_fox.shape[1]
    dff = wp["w_down"].shape[1]
    tm = _tile(seq, TM_FFN)
    nt = seq // tm
    chunk = _tile(dff, FFN_CHUNK)
    rows = lambda b, j: (b * nt + j, 0)
    wl3 = lambda b, j: (l, 0, 0)
    yspec = pl.BlockSpec((tm, bw), rows)
    st = pl.BlockSpec((None, 2, dff), lambda b, j: (b, 0, 0))
    return pl.pallas_call(
        functools.partial(_merge_ffn_kernel, d=d, dff=dff, chunk=chunk, alpha=alpha),
        grid=(nb, nt),
        in_specs=[pl.BlockSpec((tm, d), rows),
                  pl.BlockSpec((None, 1, 6 * d), lambda b, j: (b, 0, 0)),
                  yspec, yspec, yspec,
                  _resident((None, d, 3 * d), wl3),
                  _resident((None, 3, bw, d), lambda b, j: (l, 0, 0, 0)),
                  _resident((None, d, d), wl3),
                  _resident((None, 1, d), wl3),
                  _resident((None, 1, d), wl3),
                  _resident((None, d, 2 * dff), wl3),
                  _resident((None, 3, dff), wl3),
                  _resident((None, 1, dff), wl3),
                  _resident((None, dff, d), wl3),
                  _resident((None, 1, d), wl3),
                  _resident((None, 1, d), wl3),
                  st],
        out_specs=[pl.BlockSpec((tm, d), rows), st],
        out_shape=[jax.ShapeDtypeStruct((n, d), F32), jax.ShapeDtypeStruct((nb, 2, dff), F32)],
        scratch_shapes=[pltpu.VMEM((SUBLANES, dff), F32), pltpu.VMEM((tm, dff), BF16)],
        compiler_params=_params("arbitrary", "arbitrary"),
        name="merge_ffn",
    )(x2d, mod, y_fox, y_sc, y_gla, wp["w_merge"], wp["w_branch"], wp["w_out"], wp["ln1_g"], wp["ln1_b"],
      wp["w_up"], wp["ffn_conv_w"], wp["ffn_conv_b"], wp["w_down"], wp["ln2_g"], wp["ln2_b"], prev)


def _pack_weights(w_in, b_fox_f, w_gla_up, b_gla_a, gla_norm_g, sc_conv_w, sc_conv_b, w_branch, w_out,
                  ln1_g, ln1_b, w_up, ffn_conv_w, ffn_conv_b, w_down, ln2_g, ln2_b, heads):
    depth, d, _ = w_in.shape
    fw = sc_conv_w.shape[-1]
    rank = w_gla_up.shape[1]
    o = 0
    fox = w_in[:, :, o:o + 3 * fw]; o += 3 * fw
    wff = w_in[:, :, o:o + heads]; o += heads
    sc = w_in[:, :, o:o + 3 * fw]; o += 3 * fw
    gla = w_in[:, :, o:o + 4 * fw]; o += 4 * fw
    glr = w_in[:, :, o:o + rank]; o += rank
    wm = w_in[:, :, o:]
    row = lambda a: a.reshape(depth, 1, a.shape[-1])
    return {
        "fw": fw,
        "w_main": jnp.concatenate([fox, sc, gla], axis=-1).astype(BF16),
        "w_fft": jnp.swapaxes(wff, 1, 2).astype(BF16),
        "w_glr": jnp.pad(glr, ((0, 0), (0, 0), (0, LANES - rank))).astype(BF16),
        "b_fox": b_fox_f.reshape(depth, heads, 1),
        "w_gla_up": jnp.pad(w_gla_up, ((0, 0), (0, LANES - rank), (0, 0))).astype(BF16),
        "b_gla": row(b_gla_a),
        "gla_norm_g": row(gla_norm_g),
        "sc_w": sc_conv_w, "sc_b": row(sc_conv_b),
        "w_merge": wm.astype(BF16),
        "w_branch": w_branch.astype(BF16),
        "w_out": w_out.astype(BF16),
        "ln1_g": row(ln1_g), "ln1_b": row(ln1_b),
        "w_up": w_up.astype(BF16),
        "ffn_conv_w": ffn_conv_w, "ffn_conv_b": row(ffn_conv_b),
        "w_down": w_down.astype(BF16),
        "ln2_g": row(ln2_g), "ln2_b": row(ln2_b),
    }


def _layer(l, x2d, mod, wp, nb, seq, hd, alpha, fox_cache, sc_prev, gla_s0, ffn_prev, kf_all, vf_all):
    fw = wp["fw"]
    heads = fw // hd
    dk = gla_s0.shape[-2]
    prompt = fox_cache is None
    (q, kf_all, vf_all, kb, v, lft, y_sc, sc_new, gq, gk, gv, gg, la, *kn2) = _inproj(
        l, x2d, mod, wp, sc_prev, kf_all, vf_all, nb, seq, hd, dk, prompt)
    lf_t = lft.reshape(nb, -1, heads, lft.shape[-1]).transpose(2, 0, 1, 3).reshape(heads, nb, seq)
    logf = lf_t.transpose(1, 2, 0)
    if prompt:
        cp, c1, c2, c3 = _cumsum_rows(lf_t.reshape(heads * nb, seq))
        knorm = jnp.sqrt(jnp.max(kn2[0][:, 0, :heads].reshape(nb, -1, heads), axis=1))
        y_fox = _fox_prompt(cp, c1, c2, c3, knorm, q, kb, v, nb, seq, hd)
        y_gla, gla_new = _gla(l, gq, gk, gv, gg, la, gla_s0, wp["gla_norm_g"], nb, seq, False)
    else:
        cache_k, cache_v, lf_cache_nat, lf_cache_t = fox_cache
        y_fox = _fox_cached(l, q, kb, v, cache_k, cache_v, lf_cache_nat, lf_cache_t,
                            logf, lf_t.transpose(1, 0, 2), nb, seq, hd)
        y_gla, gla_new = _gla(l, gq, gk, gv, gg, la, gla_s0, wp["gla_norm_g"], nb, seq, True)
    x2, ffn_new = _merge_ffn(l, x2d, mod, y_fox, y_sc, y_gla, wp, ffn_prev, nb, seq, alpha)
    return x2, kf_all, vf_all, (logf, sc_new, gla_new, ffn_new)


def kernel(x_prompt, x_sample, c_prompt, c_sample, cache_fox_k, cache_fox_v, cache_fox_logf, state_shortconv, state_gla, state_ffn_conv, w_ada, b_ada, w_in, b_fox_f, w_gla_up, b_gla_a, gla_norm_g, sc_conv_w, sc_conv_b, w_branch, w_out, ln1_g, ln1_b, w_up, ffn_conv_w, ffn_conv_b, w_down, ln2_g, ln2_b):
    bp, sp, d = x_prompt.shape
    bs, ss, _ = x_sample.shape
    depth = w_in.shape[0]
    _, _, past, heads, hd = cache_fox_k.shape
    fw = heads * hd
    dff = ffn_conv_w.shape[-1]
    assert sc_conv_w.shape[-1] == fw and state_gla.shape[2] * state_gla.shape[3] == fw
    assert state_gla.shape[2] * state_gla.shape[4] == fw and fw % LANES == 0 and 2 * hd == LANES
    assert sc_conv_w.shape[1] == 3 and ffn_conv_w.shape[1] == 3
    alpha = (2.0 * depth) ** 0.25

    wp = _pack_weights(w_in, b_fox_f, w_gla_up, b_gla_a, gla_norm_g, sc_conv_w, sc_conv_b, w_branch, w_out,
                       ln1_g, ln1_b, w_up, ffn_conv_w, ffn_conv_b, w_down, ln2_g, ln2_b, heads)
    mods = _ada(jnp.concatenate([c_prompt, c_sample], axis=0), w_ada, b_ada)
    cache_k = cache_fox_k.transpose(0, 1, 3, 4, 2).reshape(depth, bs, fw, past)
    cache_v = cache_fox_v.transpose(0, 1, 3, 4, 2).reshape(depth, bs, fw, past)
    lf_cache_t = jnp.swapaxes(cache_fox_logf, 2, 3)

    xp = x_prompt.reshape(bp * sp, d)
    xs = x_sample.reshape(bs * ss, d)
    zeros_sc = jnp.zeros((bp, 2, fw), F32)
    zeros_gla = jnp.zeros((bp,) + state_gla.shape[2:], F32)
    zeros_ffn = jnp.zeros((bp, 2, dff), F32)
    kp_all = jnp.zeros((depth, bp, fw, sp), F32)
    vp_all = jnp.zeros((depth, bp, fw, sp), F32)
    ks_all = jnp.zeros((depth, bs * ss, fw), F32)
    vs_all = jnp.zeros((depth, bs * ss, fw), F32)
    outs_p = [[] for _ in range(4)]
    outs_s = [[] for _ in range(4)]
    for l in range(depth):
        mod_p = mods[l, :bp].reshape(bp, 1, 6 * d)
        mod_s = mods[l, bp:].reshape(bs, 1, 6 * d)
        xp, kp_all, vp_all, st_p = _layer(l, xp, mod_p, wp, bp, sp, hd, alpha, None,
                                          zeros_sc, zeros_gla, zeros_ffn, kp_all, vp_all)
        xs, ks_all, vs_all, st_s = _layer(l, xs, mod_s, wp, bs, ss, hd, alpha,
                                          (cache_k, cache_v, cache_fox_logf, lf_cache_t),
                                          state_shortconv[l], state_gla, state_ffn_conv[l], ks_all, vs_all)
        for i in range(4):
            outs_p[i].append(st_p[i])
            outs_s[i].append(st_s[i])
    stacked_p = [jnp.stack(o) for o in outs_p]
    stacked_s = [jnp.stack(o) for o in outs_s]
    kv_p = lambda a: a.reshape(depth, bp, heads, hd, sp).transpose(0, 1, 4, 2, 3)
    kv_s = lambda a: a.reshape(depth, bs, ss, heads, hd)
    return (xp.reshape(bp, sp, d), xs.reshape(bs, ss, d),
            kv_p(kp_all), kv_p(vp_all), *stacked_p,
            kv_s(ks_all), kv_s(vs_all), *stacked_s)
```

```python
import functools

import jax
import jax.numpy as jnp
from jax import lax
from jax.experimental import pallas as pl
from jax.experimental.pallas import tpu as pltpu

F32 = jnp.float32
BF16 = jnp.bfloat16
LN_EPS = 1e-5
GLA_TAU = 16.0
LANES = 128
SUBLANES = 8
VMEM_LIMIT = 56 * 1024 * 1024
HIGHEST = lax.Precision.HIGHEST
LOG2E = 1.4426950408889634
FOX_UNDERFLOW = 140.0
FOX_NORM_MARGIN = 1.01

TM_PROJ = 512
TM_FFN = 512
TQ_FOX = 512
T_GLA = 1024
GLA_CHUNK = 128
FFN_CHUNK = 256
ADA_TN = 1024

_NT = (((1,), (1,)), ((), ()))
_TN = (((0,), (0,)), ((), ()))


def _tile(n, pref):
    if n <= pref:
        return n
    t = pref
    while n % t:
        t //= 2
    return t


def _params(*sem):
    return pltpu.CompilerParams(dimension_semantics=sem, vmem_limit_bytes=VMEM_LIMIT)


def _resident(block, index_map):
    return pl.BlockSpec(block, index_map, pipeline_mode=pl.Buffered(1))


def _log_sigmoid(x):
    return jnp.minimum(x, 0.0) - jnp.log1p(jnp.exp(-jnp.abs(x)))


def _ln(x):
    mu = jnp.mean(x, axis=-1, keepdims=True)
    xc = x - mu
    var = jnp.mean(xc * xc, axis=-1, keepdims=True)
    return xc * lax.rsqrt(var + LN_EPS)


def _causal_conv3(u, p0, p1, w_ref, b_ref, cols):
    tm = u.shape[0]
    w0, w1, w2, bias = w_ref[0:1, cols], w_ref[1:2, cols], w_ref[2:3, cols], b_ref[:, cols]

    def taps(u0, u1, u2):
        y = bias + u2 * w0
        y = y + u1 * w1
        return y + u0 * w2

    head = u[0:SUBLANES, :]
    row = lax.broadcasted_iota(jnp.int32, (SUBLANES, 1), 0)
    h1 = jnp.where(row == 0, p1, pltpu.roll(head, 1, 0))
    h2 = jnp.where(row == 0, p0, jnp.where(row == 1, p1, pltpu.roll(head, 2, 0)))
    first = taps(head, h1, h2)
    if tm == SUBLANES:
        return first
    rest = taps(u, pltpu.roll(u, 1, 0), pltpu.roll(u, 2, 0))
    return jnp.concatenate([first, rest[SUBLANES:, :]], axis=0)


def _ada_kernel(c_ref, w_ref, b_ref, o_ref):
    c = c_ref[...]
    s = (c * jax.nn.sigmoid(c)).astype(BF16)
    o_ref[...] = jnp.dot(s, w_ref[...].astype(BF16), preferred_element_type=F32) + b_ref[...]


def _ada(c_all, w_ada, b_ada):
    depth, d, n6 = w_ada.shape
    r = c_all.shape[0]
    tn = _tile(n6, ADA_TN)
    return pl.pallas_call(
        _ada_kernel,
        grid=(depth, n6 // tn),
        in_specs=[pl.BlockSpec((r, d), lambda l, n: (0, 0)),
                  pl.BlockSpec((None, d, tn), lambda l, n: (l, 0, n)),
                  pl.BlockSpec((None, 1, tn), lambda l, n: (l, 0, n))],
        out_specs=pl.BlockSpec((None, r, tn), lambda l, n: (l, 0, n)),
        out_shape=jax.ShapeDtypeStruct((depth, r, n6), F32),
        compiler_params=_params("arbitrary", "arbitrary"),
        name="ada_mod",
    )(c_all, w_ada, b_ada.reshape(depth, 1, n6))


def _inproj_kernel(x_ref, mod_ref, w_ref, wff_ref, wglr_ref, bff_ref, wup_ref, ba_ref, scw_ref, scb_ref,
                   prev_ref, *rest, d, fw, hd, q_scale, gq_scale, transposed, aliased):
    rest = rest[2:] if aliased else rest
    (q_ref, kf_ref, vf_ref, kb_ref, vb_ref, lf_ref, ysc_ref, scst_ref,
     gq_ref, gk_ref, gv_ref, gg_ref, la_ref), tail = rest[:13], rest[13:]
    carry_ref = tail[-1]
    j = pl.program_id(1)
    tm = x_ref.shape[0]
    h = _ln(x_ref[...]) * (1.0 + mod_ref[:, d:2 * d]) + mod_ref[:, 0:d]
    hb = h.astype(BF16)

    def proj(g):
        return jnp.dot(hb, w_ref[:, g * fw:(g + 1) * fw], preferred_element_type=F32)

    q = proj(0) * q_scale
    k = proj(1)
    v = proj(2)
    kb_ref[...] = k.astype(BF16)
    if transposed:
        vt = v.T
        q_ref[...] = q.T.astype(BF16)
        kf_ref[...] = k.T
        vf_ref[...] = vt
        vb_ref[...] = vt.astype(BF16)
        kk = k * k
        lane = lax.broadcasted_iota(jnp.int32, (1, LANES), 1)
        norms = jnp.zeros((1, LANES), F32)
        for hh in range(fw // hd):
            n2 = jnp.sum(kk[:, hh * hd:(hh + 1) * hd], axis=-1, keepdims=True)
            norms = jnp.where(lane == hh, jnp.max(n2, axis=0, keepdims=True), norms)
        tail[0][...] = norms
    else:
        q_ref[...] = q.astype(BF16)
        kf_ref[...] = k
        vf_ref[...] = v
        vb_ref[...] = v.astype(BF16)
    fft =lax.dot_general(wff_ref[...], hb, _NT, preferred_element_type=F32)
    lf_ref[...] = _log_sigmoid(fft + bff_ref[...])

    first = j == 0
    p0 = jnp.where(first, prev_ref[0:1, :], carry_ref[SUBLANES - 2:SUBLANES - 1, :])
    p1 = jnp.where(first, prev_ref[1:2, :], carry_ref[SUBLANES - 1:SUBLANES, :])
    sb = proj(3)
    u = proj(4) * proj(5)
    y = _causal_conv3(u, p0, p1, scw_ref, scb_ref, slice(None))
    ysc_ref[...] = (sb * y).astype(BF16)
    carry_ref[...] = u[tm - SUBLANES:tm, :]
    scst_ref[...] = carry_ref[SUBLANES - 2:SUBLANES, :]

    gq_ref[...] = (proj(6) * gq_scale).astype(BF16)
    gk_ref[...] = proj(7).astype(BF16)
    gv_ref[...] = proj(8).astype(BF16)
    gg = proj(9)
    gg_ref[...] = (gg * jax.nn.sigmoid(gg)).astype(BF16)
    glr = jnp.dot(hb, wglr_ref[...], preferred_element_type=F32)
    lap = jnp.dot(glr.astype(BF16), wup_ref[...], preferred_element_type=F32) + ba_ref[...]
    la_ref[...] = _log_sigmoid(lap) / GLA_TAU


def _inproj(l, x2d, mod, wp, sc_prev, kf_all, vf_all, nb, seq, hd, dk, transposed):
    n, d = x2d.shape
    fw = wp["fw"]
    heads = fw // hd
    tm = _tile(seq, TM_PROJ)
    nt = seq // tm
    rows = lambda b, j: (b * nt + j, 0)
    wl = lambda b, j: (l, 0, 0)
    act = lambda dt: jax.ShapeDtypeStruct((n, fw), dt)
    act_spec = pl.BlockSpec((tm, fw), rows)
    if transposed:
        t_shape, t_spec = jax.ShapeDtypeStruct((fw, n), BF16), pl.BlockSpec((fw, tm), lambda b, j: (0, b * nt + j))
        q_scale = LOG2E * hd ** -0.5
        stacked = pl.BlockSpec((None, None, fw, tm), lambda b, j: (l, b, 0, j))
    else:
        t_shape, t_spec, q_scale = act(BF16), act_spec, hd ** -0.5
        stacked = pl.BlockSpec((None, tm, fw), lambda b, j: (l, b * nt + j, 0))
    hbm = pl.BlockSpec(memory_space=pl.ANY)
    aliased = not isinstance(kf_all, jax.ShapeDtypeStruct)
    stacked_in = [hbm, hbm] if aliased else []
    stacked_args = [kf_all, vf_all] if aliased else []
    kern = functools.partial(_inproj_kernel, d=d, fw=fw, hd=hd, q_scale=q_scale, gq_scale=dk ** -0.5,
                             transposed=transposed, aliased=aliased)
    extra_specs = [pl.BlockSpec((None, 1, LANES), lambda b, j: (b * nt + j, 0, 0))] if transposed else []
    extra_shapes = [jax.ShapeDtypeStruct((nb * nt, 1, LANES), F32)] if transposed else []
    return pl.pallas_call(
        kern,
        grid=(nb, nt),
        in_specs=[pl.BlockSpec((tm, d), rows),
                  pl.BlockSpec((None, 1, 6 * d), lambda b, j: (b, 0, 0)),
                  _resident((None, d, 10 * fw), wl),
                  _resident((None, heads, d), wl),
                  _resident((None, d, LANES), wl),
                  _resident((None, heads, 1), wl),
                  _resident((None, LANES, fw), wl),
                  _resident((None, 1, fw), wl),
                  _resident((None, 3, fw), wl),
                  _resident((None, 1, fw), wl),
                  pl.BlockSpec((None, 2, fw), lambda b, j: (b, 0, 0))] + stacked_in,
        out_specs=[t_spec, stacked, stacked, act_spec, t_spec,
                   pl.BlockSpec((None, heads, tm), lambda b, j: (b * nt + j, 0, 0)),
                   act_spec,
                   pl.BlockSpec((None, 2, fw), lambda b, j: (b, 0, 0)),
                   act_spec, act_spec, act_spec, act_spec, act_spec] + extra_specs,
        out_shape=[t_shape, jax.ShapeDtypeStruct(kf_all.shape, F32), jax.ShapeDtypeStruct(vf_all.shape, F32),
                   act(BF16), t_shape,
                   jax.ShapeDtypeStruct((nb * nt, heads, tm), F32),
                   act(BF16),
                   jax.ShapeDtypeStruct((nb, 2, fw), F32),
                   act(BF16), act(BF16), act(BF16), act(BF16), act(F32)] + extra_shapes,
        input_output_aliases={11: 1, 12: 2} if aliased else {},
        scratch_shapes=[pltpu.VMEM((SUBLANES, fw), F32)],
        compiler_params=_params("arbitrary", "arbitrary"),
        name="in_proj",
    )(x2d, mod, wp["w_main"], wp["w_fft"], wp["w_glr"], wp["b_fox"], wp["w_gla_up"], wp["b_gla"],
      wp["sc_w"], wp["sc_b"], sc_prev, *stacked_args)


def _lane_cumsum(x, lane):
    for k in range(7):
        s = 1 << k
        x = x + jnp.where(lane >= s, pltpu.roll(x, s, 1), 0.0)
    return x


def _cumsum_kernel(x_ref, cp_ref, c1_ref, c2_ref, c3_ref):
    r, length = x_ref.shape
    lane = lax.broadcasted_iota(jnp.int32, (r, LANES), 1)

    def body(i, carry):
        off = pl.multiple_of(i * LANES, LANES)
        x = _lane_cumsum(x_ref[:, pl.ds(off, LANES)], lane) + carry
        c = x * LOG2E
        c1 = c.astype(BF16).astype(F32)
        r1 = c - c1
        c2 = r1.astype(BF16).astype(F32)
        c3 = (r1 - c2).astype(BF16).astype(F32)
        cp_ref[:, pl.ds(off, LANES)] = c
        c1_ref[:, pl.ds(off, LANES)] = c1
        c2_ref[:, pl.ds(off, LANES)] = c2
        c3_ref[:, pl.ds(off, LANES)] = c3
        return x[:, LANES - 1:LANES]

    lax.fori_loop(0, length // LANES, body, jnp.zeros((r, 1), F32), unroll=min(8, length // LANES))


def _cumsum_rows(x):
    r, length = x.shape
    spec = pl.BlockSpec((r, length), lambda i: (0, 0))
    return pl.pallas_call(
        _cumsum_kernel,
        grid=(1,),
        in_specs=[spec],
        out_specs=[spec, spec, spec, spec],
        out_shape=[jax.ShapeDtypeStruct((r, length), F32)] * 4,
        compiler_params=_params("arbitrary"),
        name="fox_cumsum",
    )(x)


def _fox_kernel(cend_ref, qt_ref, cq1_ref, cq2_ref, cq3_ref, k_ref, ck1_ref, ck2_ref, ck3_ref, vt_ref, o_ref,
                m_ref, acc_ref, qs_ref, ak_ref, *, hd):
    b = pl.program_id(0)
    pr = pl.program_id(1)
    i = pl.program_id(2)
    tq = qt_ref.shape[1]
    seq = k_ref.shape[0]
    sub = tq // 2
    nk = seq // sub
    lane = lax.broadcasted_iota(jnp.int32, (1, LANES), 1)
    head_a = lane < hd
    cqs = (cq1_ref, cq2_ref, cq3_ref)
    cks = (ck1_ref, ck2_ref, ck3_ref)

    group = 2 * SUBLANES

    @pl.when(i == 0)
    def _():
        rk = lax.broadcasted_iota(jnp.int32, (group, LANES), 0)
        gap = jnp.zeros((hd - group, LANES), F32)

        def chunk(t, carry):
            off = pl.multiple_of(t * LANES, LANES)
            parts = []
            for h in (1, 0):
                rows = jnp.where(rk < 3, 1.0, 0.0)
                for n, ck in enumerate(cks):
                    rows = jnp.where(rk == 3 + n, -ck[h, :, pl.ds(off, LANES)], rows)
                parts += [rows, gap]
            ak_ref[pl.ds(off, LANES), :] = jnp.concatenate(parts, axis=0).T.astype(BF16)
            return carry

        lax.fori_loop(0, seq // LANES, chunk, 0, unroll=min(4, seq // LANES))

    rg = lax.broadcasted_iota(jnp.int32, (group, tq), 0)

    def bias_rows(h):
        rows = jnp.where(rg < 6, 1.0, 0.0)
        for n, cq in enumerate(cqs):
            rows = jnp.where(rg == n, cq[h], rows)
        return rows.astype(BF16)

    qs_ref[0, 0:hd, :] = qt_ref[0:hd, :]
    qs_ref[0, hd:hd + group, :] = bias_rows(0)
    qs_ref[0, hd + group:, :] = jnp.zeros((hd - group, tq), BF16)
    qs_ref[1, 0:group, :] = bias_rows(1)
    qs_ref[1, group:hd, :] = jnp.zeros((hd - group, tq), BF16)
    qs_ref[1, hd:, :] = qt_ref[hd:, :]
    m_ref[...] = jnp.full(m_ref.shape, -jnp.inf, F32)
    acc_ref[...] = jnp.zeros(acc_ref.shape, F32)

    def step(jk, tk, diagonal):
        off = pl.multiple_of(jk * sub, sub)
        ones = jnp.ones((hd, tk), BF16)
        k = k_ref[pl.ds(off, tk), :]
        ak = ak_ref[pl.ds(off, tk), :]
        ks = (jnp.where(head_a, k, ak), jnp.where(head_a, ak, k))
        vts = (jnp.concatenate([vt_ref[0:hd, pl.ds(off, tk)], ones], axis=0),
               jnp.concatenate([ones, vt_ref[hd:, pl.ds(off, tk)]], axis=0))
        sts = [jnp.dot(ks[h], qs_ref[h], preferred_element_type=F32) for h in range(2)]
        if diagonal:
            krow = lax.broadcasted_iota(jnp.int32, (tk, tq), 0)
            qcol = lax.broadcasted_iota(jnp.int32, (tk, tq), 1)
            sts = [jnp.where(krow <= qcol, st, -jnp.inf) for st in sts]
        m_prevs = [m_ref[h] for h in range(2)]
        m_news = [jnp.maximum(m_prevs[h], jnp.max(sts[h], axis=0, keepdims=True)) for h in range(2)]
        for h in range(2):
            p = jnp.exp2(sts[h] - m_news[h]).astype(BF16)
            alpha = jnp.exp2(m_prevs[h] - m_news[h])
            acc_ref[h] = alpha * acc_ref[h] + jnp.dot(vts[h], p, preferred_element_type=F32)
            m_ref[h] = m_news[h]

    first = 2 * i
    step(first, tq, True)

    pair = b * pl.num_programs(1) + pr
    base = pair * 2 * nk
    kmax_base = pl.num_programs(0) * pl.num_programs(1) * 2 * nk + pair * 2
    slack = []
    for h in range(2):
        qf = qt_ref[h * hd:(h + 1) * hd, :].astype(F32)
        qn = jnp.sqrt(jnp.max(jnp.sum(qf * qf, axis=0, keepdims=True)))
        slack.append(qn * cend_ref[kmax_base + h] * FOX_NORM_MARGIN - jnp.min(m_ref[h]))
    before = jnp.maximum(first - 1, 0)

    def needed(j):
        r = False
        for h in range(2):
            gap = cend_ref[base + h * nk + before] - cend_ref[base + h * nk + j]
            r = jnp.logical_or(r, slack[h] + gap > -FOX_UNDERFLOW)
        return r

    j_lo = lax.while_loop(lambda j: jnp.logical_and(j > 0, needed(jnp.maximum(j - 1, 0))), lambda j: j - 1,
                          first)
    n_off = first - j_lo
    rem = n_off % 4
    for r in (1, 2, 3):
        @pl.when(rem == r)
        def _(r=r):
            step(j_lo, r * sub, False)

    def body(t, carry):
        step(j_lo + rem + 4 * t, 4 * sub, False)
        return carry

    lax.fori_loop(0, n_off // 4, body, 0)

    acc_a = acc_ref[0]
    acc_b = acc_ref[1]
    out_t = jnp.concatenate([acc_a[0:hd] / acc_a[hd:hd + 1], acc_b[hd:] / acc_b[0:1]], axis=0)
    o_ref[...] = out_t.T.astype(BF16)


def _fox_prompt(cp, c1, c2, c3, knorm, qt, kb, vt, nb, seq, hd):
    fw, n = qt.shape
    pairs = fw // LANES
    tq = _tile(seq, TQ_FOX)
    nq = seq // tq
    by_pair = lambda a: a.reshape(pairs, 2, nb, 1, seq)
    sub = tq // 2
    cend = by_pair(cp)[:, :, :, 0, sub - 1::sub].transpose(2, 0, 1, 3).reshape(-1)
    cend = jnp.concatenate([cend, knorm.reshape(-1)])
    terms = [by_pair(c) for c in (c1, c2, c3)]
    qmap = lambda b, p, i, c: (p, b * nq + i)
    kmap = lambda b, p, i, c: (b, p)
    cq_spec = pl.BlockSpec((None, 2, None, 1, tq), lambda b, p, i, c: (p, 0, b, 0, i))
    ck_spec = pl.BlockSpec((None, 2, None, 1, seq), lambda b, p, i, c: (p, 0, b, 0, 0))
    grid_spec = pltpu.PrefetchScalarGridSpec(
        num_scalar_prefetch=1,
        grid=(nb, pairs, nq),
        in_specs=[pl.BlockSpec((LANES, tq), qmap), cq_spec, cq_spec, cq_spec,
                  pl.BlockSpec((seq, LANES), kmap), ck_spec, ck_spec, ck_spec,
                  pl.BlockSpec((LANES, seq), lambda b, p, i, c: (p, b))],
        out_specs=pl.BlockSpec((tq, LANES), lambda b, p, i, c: (b * nq + i, p)),
        scratch_shapes=[pltpu.VMEM((2, 1, tq), F32), pltpu.VMEM((2, LANES, tq), F32),
                        pltpu.VMEM((2, LANES, tq), BF16), pltpu.VMEM((seq, LANES), BF16)],
    )
    return pl.pallas_call(
        functools.partial(_fox_kernel, hd=hd),
        grid_spec=grid_spec,
        out_shape=jax.ShapeDtypeStruct((n, fw), BF16),
        compiler_params=_params("arbitrary", "arbitrary", "arbitrary"),
        name="fox_attn",
    )(cend, qt, *terms, kb, *terms, vt)


def _fox_cached_kernel(q_ref, kn_ref, vn_ref, kc_ref, vc_ref, lcn_ref, lct_ref, lnn_ref, lnt_ref, o_ref,
                       ck_ref, *, hd):
    ln, fw = q_ref.shape
    past = kc_ref.shape[1]
    heads = fw // hd
    lane8 = lax.broadcasted_iota(jnp.int32, (heads, LANES), 1)

    def body(i, carry):
        off = pl.multiple_of(i * LANES, LANES)
        x = _lane_cumsum(lct_ref[:, pl.ds(off, LANES)], lane8) + carry
        ck_ref[:, pl.ds(off, LANES)] = x
        return x[:, LANES - 1:LANES]

    c_last = lax.fori_loop(0, past // LANES, body, jnp.zeros((heads, 1), F32),
                           unroll=min(8, past // LANES))
    r_i = lax.broadcasted_iota(jnp.int32, (ln, ln), 0)
    c_i = lax.broadcasted_iota(jnp.int32, (ln, ln), 1)
    causal = c_i <= r_i
    tril = causal.astype(F32)
    triu = (r_i <= c_i).astype(F32)
    ck_new = c_last + jnp.dot(lnt_ref[...], triu, precision=HIGHEST, preferred_element_type=F32)
    cache_total = jnp.dot(jnp.ones((ln, past), F32), lcn_ref[...], precision=HIGHEST,
                          preferred_element_type=F32)
    cq = cache_total + jnp.dot(tril, lnn_ref[...], precision=HIGHEST, preferred_element_type=F32)

    outs = []
    for h in range(heads):
        feat = slice(h * hd, (h + 1) * hd)
        qh = q_ref[:, feat]
        kct = kc_ref[feat, :].astype(BF16)
        vct = vc_ref[feat, :].astype(BF16)
        cqh = cq[:, h:h + 1]
        s_c = jnp.dot(qh, kct, preferred_element_type=F32) + cqh - ck_ref[h:h + 1, :]
        s_n = lax.dot_general(qh, kn_ref[:, feat], _NT, preferred_element_type=F32) + cqh - ck_new[h:h + 1, :]
        s_n = jnp.where(causal, s_n, -jnp.inf)
        m = jnp.maximum(jnp.max(s_c, axis=-1, keepdims=True), jnp.max(s_n, axis=-1, keepdims=True))
        p_c = jnp.exp(s_c - m)
        p_n = jnp.exp(s_n - m)
        den = jnp.sum(p_c, axis=-1, keepdims=True) + jnp.sum(p_n, axis=-1, keepdims=True)
        acc = (lax.dot_general(p_c.astype(BF16), vct, _NT, preferred_element_type=F32)
               + jnp.dot(p_n.astype(BF16), vn_ref[:, feat], preferred_element_type=F32))
        outs.append(acc / den)
    o_ref[...] = jnp.concatenate(outs, axis=1).astype(BF16)


def _fox_cached(l, qb, kb, vb, cache_k, cache_v, lf_cache_nat, lf_cache_t, lf_new_nat, lf_new_t, nb, ln, hd):
    n, fw = qb.shape
    past = cache_k.shape[3]
    heads = fw // hd
    rows = lambda b: (b, 0)
    cache = lambda b: (l, b, 0, 0)
    act = pl.BlockSpec((ln, fw), rows)
    return pl.pallas_call(
        functools.partial(_fox_cached_kernel, hd=hd),
        grid=(nb,),
        in_specs=[act, act, act,
                  pl.BlockSpec((None, None, fw, past), cache),
                  pl.BlockSpec((None, None, fw, past), cache),
                  pl.BlockSpec((None, None, past, heads), cache),
                  pl.BlockSpec((None, None, heads, past), cache),
                  pl.BlockSpec((None, ln, heads), lambda b: (b, 0, 0)),
                  pl.BlockSpec((None, heads, ln), lambda b: (b, 0, 0))],
        out_specs=act,
        out_shape=jax.ShapeDtypeStruct((n, fw), BF16),
        scratch_shapes=[pltpu.VMEM((heads, past), F32)],
        compiler_params=_params("arbitrary"),
        name="fox_attn_cached",
    )(qb, kb, vb, cache_k, cache_v, lf_cache_nat, lf_cache_t, lf_new_nat, lf_new_t)


def _gla_kernel(q_ref, k_ref, v_ref, g_ref, la_ref, s0_ref, gn_ref, y_ref, sout_ref, st_ref,
                *, chunk, nh, dk, dv):
    j = pl.program_id(1)
    t_rows = q_ref.shape[0]

    @pl.when(j == 0)
    def _():
        for h in range(nh):
            st_ref[h] = s0_ref[h].T

    row = lax.broadcasted_iota(jnp.int32, (chunk, chunk), 0)
    col = lax.broadcasted_iota(jnp.int32, (chunk, chunk), 1)
    causal = col <= row
    tril = causal.astype(F32)
    mid = chunk // 2
    for c in range(t_rows // chunk):
        rows = slice(c * chunk, (c + 1) * chunk)
        for h in range(nh):
            kc = slice(h * dk, (h + 1) * dk)
            vc = slice(h * dv, (h + 1) * dv)
            b = jnp.dot(tril, la_ref[rows, kc], precision=HIGHEST, preferred_element_type=F32)
            b_last = b[chunk - 1:chunk, :]
            b_mid = b[mid:mid + 1, :]
            q = q_ref[rows, kc].astype(F32)
            k = k_ref[rows, kc].astype(F32)
            v = v_ref[rows, vc]
            qi = (q * jnp.exp(b - b_mid)).astype(BF16)
            ki = (k * jnp.exp(b_mid - b)).astype(BF16)
            a = lax.dot_general(qi, ki, _NT, preferred_element_type=F32)
            a = jnp.where(causal, a, 0.0)
            o = jnp.dot(a.astype(BF16), v, preferred_element_type=F32)
            st = st_ref[h]
            qe = (q * jnp.exp(b)).astype(BF16)
            o = o + lax.dot_general(qe, st.astype(BF16), _NT, preferred_element_type=F32)
            kd = (k * jnp.exp(b_last - b)).astype(BF16)
            st_ref[h] = st * jnp.exp(b_last) + lax.dot_general(v, kd, _TN, preferred_element_type=F32)
            ms = jnp.mean(o * o, axis=-1, keepdims=True)
            y = o * lax.rsqrt(ms + LN_EPS) * gn_ref[...]
            y_ref[rows, vc] = (y * g_ref[rows, vc].astype(F32)).astype(BF16)

    @pl.when(j == pl.num_programs(1) - 1)
    def _():
        for h in range(nh):
            sout_ref[h] = st_ref[h].T


def _gla(l, gq, gk, gv, gg, la, s0, gn, nb, seq, s0_layered):
    n, kw = gq.shape
    nh, dk, dv = s0.shape[-3:]
    t_rows = _tile(seq, T_GLA)
    chunk = _tile(t_rows, GLA_CHUNK)
    nt = seq // t_rows
    rows = lambda b, j: (b * nt + j, 0)
    act = pl.BlockSpec((t_rows, kw), rows)
    if s0_layered:
        s0_spec = pl.BlockSpec((None, None, nh, dk, dv), lambda b, j: (l, b, 0, 0, 0))
    else:
        s0_spec = pl.BlockSpec((None, nh, dk, dv), lambda b, j: (b, 0, 0, 0))
    return pl.pallas_call(
        functools.partial(_gla_kernel, chunk=chunk, nh=nh, dk=dk, dv=dv),
        grid=(nb, nt),
        in_specs=[act, act, act, act, act, s0_spec,
                  _resident((None, 1, dv), lambda b, j: (l, 0, 0))],
        out_specs=[act, pl.BlockSpec((None, nh, dk, dv), lambda b, j: (b, 0, 0, 0))],
        out_shape=[jax.ShapeDtypeStruct((n, kw), BF16), jax.ShapeDtypeStruct((nb, nh, dk, dv), F32)],
        scratch_shapes=[pltpu.VMEM((nh, dv, dk), F32)],
        compiler_params=_params("arbitrary", "arbitrary"),
        name="gla",
    )(gq, gk, gv, gg, la, s0, gn)


def _merge_ffn_kernel(x_ref, mod_ref, yf_ref, ys_ref, yg_ref, wm_ref, wb_ref, wo_ref, g1_ref, b1_ref,
                      wu_ref, cw_ref, cb_ref, wd_ref, g_ref, b_ref, prev_ref, o_ref, st_ref,
                      carry_ref, act_ref, *, d, dff, chunk, alpha):
    j = pl.program_id(1)
    tm = x_ref.shape[0]
    first = j == 0
    x0 = x_ref[...]
    hb = (_ln(x0) * (1.0 + mod_ref[:, d:2 * d]) + mod_ref[:, 0:d]).astype(BF16)
    merged = None
    for i, y_ref in enumerate((yf_ref, ys_ref, yg_ref)):
        gate = jax.nn.sigmoid(jnp.dot(hb, wm_ref[:, i * d:(i + 1) * d], preferred_element_type=F32))
        term = gate * jnp.dot(y_ref[...], wb_ref[i], preferred_element_type=F32)
        merged = term if merged is None else merged + term
    mix = jnp.dot(merged.astype(BF16), wo_ref[...], preferred_element_type=F32)
    x = _ln(alpha * x0 + mod_ref[:, 2 * d:3 * d] * mix) * g1_ref[...] + b1_ref[...]

    hb = (_ln(x) * (1.0 + mod_ref[:, 4 * d:5 * d]) + mod_ref[:, 3 * d:4 * d]).astype(BF16)
    for c0 in range(0, dff, chunk):
        cols = slice(c0, c0 + chunk)
        ug = jnp.dot(hb, wu_ref[:, cols], preferred_element_type=F32)
        uv = jnp.dot(hb, wu_ref[:, dff + c0:dff + c0 + chunk], preferred_element_type=F32)
        p0 = jnp.where(first, prev_ref[0:1, cols], carry_ref[SUBLANES - 2:SUBLANES - 1, cols])
        p1 = jnp.where(first, prev_ref[1:2, cols], carry_ref[SUBLANES - 1:SUBLANES, cols])
        ugc = _causal_conv3(ug, p0, p1, cw_ref, cb_ref, cols)
        carry_ref[:, cols] = ug[tm - SUBLANES:tm, :]
        act_ref[:, cols] = (jax.nn.gelu(ugc) * uv).astype(BF16)
    f = jnp.dot(act_ref[...], wd_ref[...], preferred_element_type=F32)
    st_ref[...] = carry_ref[SUBLANES - 2:SUBLANES, :]
    z = alpha * x + mod_ref[:, 5 * d:6 * d] * f
    o_ref[...] = _ln(z) * g_ref[...] + b_ref[...]


def _merge_ffn(l, x2d, mod, y_fox, y_sc, y_gla, wp, prev, nb, seq, alpha):
    n, d = x2d.shape
    bw = y_fox.shape[1]
    dff = wp["w_down"].shape[1]
    tm = _tile(seq, TM_FFN)
    nt = seq // tm
    chunk = _tile(dff, FFN_CHUNK)
    rows = lambda b, j: (b * nt + j, 0)
    wl3 = lambda b, j: (l, 0, 0)
    yspec = pl.BlockSpec((tm, bw), rows)
    st = pl.BlockSpec((None, 2, dff), lambda b, j: (b, 0, 0))
    return pl.pallas_call(
        functools.partial(_merge_ffn_kernel, d=d, dff=dff, chunk=chunk, alpha=alpha),
        grid=(nb, nt),
        in_specs=[pl.BlockSpec((tm, d), rows),
                  pl.BlockSpec((None, 1, 6 * d), lambda b, j: (b, 0, 0)),
                  yspec, yspec, yspec,
                  _resident((None, d, 3 * d), wl3),
                  _resident((None, 3, bw, d), lambda b, j: (l, 0, 0, 0)),
                  _resident((None, d, d), wl3),
                  _resident((None, 1, d), wl3),
                  _resident((None, 1, d), wl3),
                  _resident((None, d, 2 * dff), wl3),
                  _resident((None, 3, dff), wl3),
                  _resident((None, 1, dff), wl3),
                  _resident((None, dff, d), wl3),
                  _resident((None, 1, d), wl3),
                  _resident((None, 1, d), wl3),
                  st],
        out_specs=[pl.BlockSpec((tm, d), rows), st],
        out_shape=[jax.ShapeDtypeStruct((n, d), F32), jax.ShapeDtypeStruct((nb, 2, dff), F32)],
        scratch_shapes=[pltpu.VMEM((SUBLANES, dff), F32), pltpu.VMEM((tm, dff), BF16)],
        compiler_params=_params("arbitrary", "arbitrary"),
        name="merge_ffn",
    )(x2d, mod, y_fox, y_sc, y_gla, wp["w_merge"], wp["w_branch"], wp["w_out"], wp["ln1_g"], wp["ln1_b"],
      wp["w_up"], wp["ffn_conv_w"], wp["ffn_conv_b"], wp["w_down"], wp["ln2_g"], wp["ln2_b"], prev)


def _pack_weights(w_in, b_fox_f, w_gla_up, b_gla_a, gla_norm_g, sc_conv_w, sc_conv_b, w_branch, w_out,
                  ln1_g, ln1_b, w_up, ffn_conv_w, ffn_conv_b, w_down, ln2_g, ln2_b, heads):
    depth, d, _ = w_in.shape
    fw = sc_conv_w.shape[-1]
    rank = w_gla_up.shape[1]
    o = 0
    fox = w_in[:, :, o:o + 3 * fw]; o += 3 * fw
    wff = w_in[:, :, o:o + heads]; o += heads
    sc = w_in[:, :, o:o + 3 * fw]; o += 3 * fw
    gla = w_in[:, :, o:o + 4 * fw]; o += 4 * fw
    glr = w_in[:, :, o:o + rank]; o += rank
    wm = w_in[:, :, o:]
    row = lambda a: a.reshape(depth, 1, a.shape[-1])
    return {
        "fw": fw,
        "w_main": jnp.concatenate([fox, sc, gla], axis=-1).astype(BF16),
        "w_fft": jnp.swapaxes(wff, 1, 2).astype(BF16),
        "w_glr": jnp.pad(glr, ((0, 0), (0, 0), (0, LANES - rank))).astype(BF16),
        "b_fox": b_fox_f.reshape(depth, heads, 1),
        "w_gla_up": jnp.pad(w_gla_up, ((0, 0), (0, LANES - rank), (0, 0))).astype(BF16),
        "b_gla": row(b_gla_a),
        "gla_norm_g": row(gla_norm_g),
        "sc_w": sc_conv_w, "sc_b": row(sc_conv_b),
        "w_merge": wm.astype(BF16),
        "w_branch": w_branch.astype(BF16),
        "w_out": w_out.astype(BF16),
        "ln1_g": row(ln1_g), "ln1_b": row(ln1_b),
        "w_up": w_up.astype(BF16),
        "ffn_conv_w": ffn_conv_w, "ffn_conv_b": row(ffn_conv_b),
        "w_down": w_down.astype(BF16),
        "ln2_g": row(ln2_g), "ln2_b": row(ln2_b),
    }


def _layer(l, x2d, mod, wp, nb, seq, hd, alpha, fox_cache, sc_prev, gla_s0, ffn_prev, kf_all, vf_all):
    fw = wp["fw"]
    heads = fw // hd
    dk = gla_s0.shape[-2]
    prompt = fox_cache is None
    (q, kf_all, vf_all, kb, v, lft, y_sc, sc_new, gq, gk, gv, gg, la, *kn2) = _inproj(
        l, x2d, mod, wp, sc_prev, kf_all, vf_all, nb, seq, hd, dk, prompt)
    lf_t = lft.reshape(nb, -1, heads, lft.shape[-1]).transpose(2, 0, 1, 3).reshape(heads, nb, seq)
    logf = lf_t.transpose(1, 2, 0)
    if prompt:
        cp, c1, c2, c3 = _cumsum_rows(lf_t.reshape(heads * nb, seq))
        knorm = jnp.sqrt(jnp.max(kn2[0][:, 0, :heads].reshape(nb, -1, heads), axis=1))
        y_fox = _fox_prompt(cp, c1, c2, c3, knorm, q, kb, v, nb, seq, hd)
        y_gla, gla_new = _gla(l, gq, gk, gv, gg, la, gla_s0, wp["gla_norm_g"], nb, seq, False)
    else:
        cache_k, cache_v, lf_cache_nat, lf_cache_t = fox_cache
        y_fox = _fox_cached(l, q, kb, v, cache_k, cache_v, lf_cache_nat, lf_cache_t,
                            logf, lf_t.transpose(1, 0, 2), nb, seq, hd)
        y_gla, gla_new = _gla(l, gq, gk, gv, gg, la, gla_s0, wp["gla_norm_g"], nb, seq, True)
    x2, ffn_new = _merge_ffn(l, x2d, mod, y_fox, y_sc, y_gla, wp, ffn_prev, nb, seq, alpha)
    return x2, kf_all, vf_all, (logf, sc_new, gla_new, ffn_new)


def kernel(x_prompt, x_sample, c_prompt, c_sample, cache_fox_k, cache_fox_v, cache_fox_logf, state_shortconv, state_gla, state_ffn_conv, w_ada, b_ada, w_in, b_fox_f, w_gla_up, b_gla_a, gla_norm_g, sc_conv_w, sc_conv_b, w_branch, w_out, ln1_g, ln1_b, w_up, ffn_conv_w, ffn_conv_b, w_down, ln2_g, ln2_b):
    bp, sp, d = x_prompt.shape
    bs, ss, _ = x_sample.shape
    depth = w_in.shape[0]
    _, _, past, heads, hd = cache_fox_k.shape
    fw = heads * hd
    dff = ffn_conv_w.shape[-1]
    assert sc_conv_w.shape[-1] == fw and state_gla.shape[2] * state_gla.shape[3] == fw
    assert state_gla.shape[2] * state_gla.shape[4] == fw and fw % LANES == 0 and 2 * hd == LANES
    assert sc_conv_w.shape[1] == 3 and ffn_conv_w.shape[1] == 3
    alpha = (2.0 * depth) ** 0.25

    wp = _pack_weights(w_in, b_fox_f, w_gla_up, b_gla_a, gla_norm_g, sc_conv_w, sc_conv_b, w_branch, w_out,
                       ln1_g, ln1_b, w_up, ffn_conv_w, ffn_conv_b, w_down, ln2_g, ln2_b, heads)
    mods = _ada(jnp.concatenate([c_prompt, c_sample], axis=0), w_ada, b_ada)
    cache_k = cache_fox_k.transpose(0, 1, 3, 4, 2).reshape(depth, bs, fw, past)
    cache_v = cache_fox_v.transpose(0, 1, 3, 4, 2).reshape(depth, bs, fw, past)
    lf_cache_t = jnp.swapaxes(cache_fox_logf, 2, 3)

    xp = x_prompt.reshape(bp * sp, d)
    xs = x_sample.reshape(bs * ss, d)
    zeros_sc = jnp.zeros((bp, 2, fw), F32)
    zeros_gla = jnp.zeros((bp,) + state_gla.shape[2:], F32)
    zeros_ffn = jnp.zeros((bp, 2, dff), F32)
    kp_all = vp_all = jax.ShapeDtypeStruct((depth, bp, fw, sp), F32)
    ks_all = vs_all = jax.ShapeDtypeStruct((depth, bs * ss, fw), F32)
    outs_p = [[] for _ in range(4)]
    outs_s = [[] for _ in range(4)]
    for l in range(depth):
        mod_p = mods[l, :bp].reshape(bp, 1, 6 * d)
        mod_s = mods[l, bp:].reshape(bs, 1, 6 * d)
        xp, kp_all, vp_all, st_p = _layer(l, xp, mod_p, wp, bp, sp, hd, alpha, None,
                                          zeros_sc, zeros_gla, zeros_ffn, kp_all, vp_all)
        xs, ks_all, vs_all, st_s = _layer(l, xs, mod_s, wp, bs, ss, hd, alpha,
                                          (cache_k, cache_v, cache_fox_logf, lf_cache_t),
                                          state_shortconv[l], state_gla, state_ffn_conv[l], ks_all, vs_all)
        for i in range(4):
            outs_p[i].append(st_p[i])
            outs_s[i].append(st_s[i])
    stacked_p = [jnp.stack(o) for o in outs_p]
    stacked_s = [jnp.stack(o) for o in outs_s]
    kv_p = lambda a: a.reshape(depth, bp, heads, hd, sp).transpose(0, 1, 4, 2, 3)
    kv_s = lambda a: a.reshape(depth, bs, ss, heads, hd)
    return (xp.reshape(bp, sp, d), xs.reshape(bs, ss, d),
            kv_p(kp_all), kv_p(vp_all), *stacked_p,
            kv_s(ks_all), kv_s(vs_all), *stacked_s)
```
